```python
import jax, jax.numpy as jnp
from jax import lax
import numpy as np

D_MODEL = 4096
BATCH = 2
SEQ = 8192
DEPTH = 1
DEC_BATCH = 32
DEC_SEQ = 32
PAST_LEN = 4096

CHUNK = 64
N_PAST_CHUNKS = 8
ATT_CTX = N_PAST_CHUNKS * CHUNK
BAND = ATT_CTX + CHUNK

D_MIX = D_MODEL
D_RWKV = D_MIX // 2
RWKV_HEAD = 64
H_RWKV = D_RWKV // RWKV_HEAD
D_ATT = D_MIX - D_RWKV
ATT_HEAD = 128
H_ATT = D_ATT // ATT_HEAD

DECAY_LORA = max(32, int(round(1.8 * D_RWKV ** 0.5 / 32)) * 32)
A_LORA = max(32, int(round(1.8 * D_RWKV ** 0.5 / 32)) * 32)
GATE_LORA = max(32, int(round(0.6 * D_RWKV ** 0.8 / 32)) * 32)
N_RC = 3 * D_RWKV + DECAY_LORA + A_LORA + GATE_LORA
N_IN = N_RC + 3 * D_ATT

REL_CLIP = 128
D_FF = ((8 * D_MODEL // 3 + 255) // 256) * 256
FFN_CONV = 3
RMS_EPS = 1e-6
GN_EPS = 64e-5
NEG_INF = -1e30

kernel_name = "hymba_rwkv7_chunkband_convffn_step"


def _rms(x, g):
    xf = x.astype(jnp.float32)
    y = xf * lax.rsqrt(jnp.mean(xf * xf, axis=-1, keepdims=True) + RMS_EPS)
    return (y * g.astype(jnp.float32)).astype(x.dtype)


def _wkv7_scan(r, w, k, v, a, b, s0):
    def step(s, inp):
        r_t, w_t, k_t, v_t, a_t, b_t = inp
        sa = jnp.einsum('bhij,bhj->bhi', s, a_t)
        s = s * w_t[:, :, None, :] + sa[..., None] * b_t[:, :, None, :] + v_t[..., None] * k_t[:, :, None, :]
        return s, jnp.einsum('bhij,bhj->bhi', s, r_t)
    xs = tuple(jnp.swapaxes(t, 0, 1) for t in (r, w, k, v, a, b))
    s_final, ys = lax.scan(step, s0, xs)
    return jnp.swapaxes(ys, 0, 1), s_final


def _rwkv7(pr, shift_prev, s0, mu, w0, w2, a0, a2, g2, k_k, k_a, r_k, ln_w, ln_b):
    B, T, _ = pr.shape
    f32 = jnp.float32
    shifted = jnp.concatenate([shift_prev.astype(pr.dtype), pr[:, :-1]], axis=1)
    xm = (pr + (shifted - pr) * mu).astype(f32)
    r = xm[..., :D_RWKV]
    k = xm[..., D_RWKV:2 * D_RWKV]
    v = xm[..., 2 * D_RWKV:3 * D_RWKV]
    o = 3 * D_RWKV
    xw = xm[..., o:o + DECAY_LORA]
    o += DECAY_LORA
    xa = xm[..., o:o + A_LORA]
    o += A_LORA
    xg = xm[..., o:o + GATE_LORA]
    w = -jax.nn.softplus(-(w0.astype(f32) + jnp.tanh(xw) @ w2.astype(f32))) - 0.5
    decay = jnp.exp(-jnp.exp(w))
    a = jax.nn.sigmoid(a0.astype(f32) + xa @ a2.astype(f32))
    g = jax.nn.sigmoid(xg) @ g2.astype(f32)
    hv = lambda t: t.reshape(B, T, H_RWKV, RWKV_HEAD)
    kk = hv(k * k_k.astype(f32))
    kk = kk * lax.rsqrt(jnp.maximum(jnp.sum(kk * kk, axis=-1, keepdims=True), 1e-24))
    k = k * (1.0 + (a - 1.0) * k_a.astype(f32))
    rh, kh, vh = hv(r), hv(k), hv(v)
    y, s_final = _wkv7_scan(rh, hv(decay), kh, vh, -kk, kk * hv(a), s0.astype(f32))
    mean = jnp.mean(y, axis=-1, keepdims=True)
    var = jnp.mean(jnp.square(y - mean), axis=-1, keepdims=True)
    y = ((y - mean) * lax.rsqrt(var + GN_EPS)).reshape(B, T, D_RWKV) * ln_w.astype(f32) + ln_b.astype(f32)
    bonus = jnp.sum(rh * kh * r_k.astype(f32), axis=-1, keepdims=True) * vh
    out = (y + bonus.reshape(B, T, D_RWKV)) * g
    return out, pr[:, -1:], s_final


def _att_qkv(pa, q_g, k_g):
    B, T, _ = pa.shape
    q = pa[..., :D_ATT].reshape(B, T, H_ATT, ATT_HEAD)
    k = pa[..., D_ATT:2 * D_ATT].reshape(B, T, H_ATT, ATT_HEAD)
    v = pa[..., 2 * D_ATT:].reshape(B, T, H_ATT, ATT_HEAD)
    return _rms(q, q_g), _rms(k, k_g), v


def _rel_bias(table, rel):
    idx = jnp.clip(rel, -REL_CLIP, REL_CLIP) + REL_CLIP
    return jnp.moveaxis(table[idx].astype(jnp.float32), -1, 0)


def _band_attention(q, k, v, table):
    B, T, H, Dh = q.shape
    nc = T // CHUNK
    f32 = jnp.float32
    k_pad = jnp.concatenate([jnp.zeros((B, ATT_CTX, H, Dh), k.dtype), k], axis=1)
    v_pad = jnp.concatenate([jnp.zeros((B, ATT_CTX, H, Dh), v.dtype), v], axis=1)
    qi = jnp.arange(CHUNK)[:, None]
    kj = jnp.arange(BAND)[None, :]
    bias = _rel_bias(table, kj - ATT_CTX - qi)
    scale = ATT_HEAD ** -0.5

    def one_chunk(c):
        start = c * CHUNK
        qc = lax.dynamic_slice_in_dim(q, start, CHUNK, axis=1).astype(f32)
        kc = lax.dynamic_slice_in_dim(k_pad, start, BAND, axis=1).astype(f32)
        vc = lax.dynamic_slice_in_dim(v_pad, start, BAND, axis=1)
        s = jnp.einsum('bqhd,bkhd->bhqk', qc, kc) * scale + bias
        valid = (start - ATT_CTX + jnp.arange(BAND)) >= 0
        s = jnp.where(valid[None, None, None, :], s, NEG_INF)
        p = jax.nn.softmax(s, axis=-1)
        return jnp.einsum('bhqk,bkhd->bqhd', p, vc.astype(f32))

    o = lax.map(one_chunk, jnp.arange(nc))
    return jnp.moveaxis(o, 0, 1).reshape(B, T, H * Dh)


def _cached_attention(q, k, v, k_cache, v_cache, table):
    B, T, H, Dh = q.shape
    L = k_cache.shape[1]
    f32 = jnp.float32
    kk = jnp.concatenate([k_cache.astype(f32), k.astype(f32)], axis=1)
    vv = jnp.concatenate([v_cache.astype(f32), v.astype(f32)], axis=1)
    rel = jnp.arange(L + T)[None, :] - L - jnp.arange(T)[:, None]
    s = jnp.einsum('bqhd,bkhd->bhqk', q.astype(f32), kk) * (ATT_HEAD ** -0.5) + _rel_bias(table, rel)
    p = jax.nn.softmax(s, axis=-1)
    return jnp.einsum('bhqk,bkhd->bqhd', p, vv).reshape(B, T, H * Dh)


def _conv_ffn(h, conv_prev, w_up, conv_w, conv_b, w_down):
    T = h.shape[1]
    up = h @ w_up
    gate, val = up[..., :D_FF], up[..., D_FF:]
    gp = jnp.concatenate([conv_prev.astype(gate.dtype), gate], axis=1)
    conv = conv_b
    for j in range(FFN_CONV):
        conv = conv + gp[:, j:j + T] * conv_w[j]
    y = (jax.nn.silu(conv) * val) @ w_down
    return y, gp[:, -(FFN_CONV - 1):]


def _block(x, shift_prev, wkv_prev, conv_prev, k_cache, v_cache, lp):
    (attn_norm_g, w_in, rwkv_mu, rwkv_w0, rwkv_w2, rwkv_a0, rwkv_a2, rwkv_g2, rwkv_k_k, rwkv_k_a,
     rwkv_r_k, rwkv_ln_w, rwkv_ln_b, att_q_norm_g, att_k_norm_g, att_rel_bias, w_out,
     ffn_norm_g, ffn_w_up, ffn_conv_w, ffn_conv_b, ffn_w_down) = lp
    T = x.shape[1]
    xn = _rms(x, attn_norm_g)
    proj = xn @ w_in
    pr, pa = proj[..., :N_RC], proj[..., N_RC:]
    y_r, shift_new, wkv_new = _rwkv7(pr, shift_prev, wkv_prev, rwkv_mu, rwkv_w0, rwkv_w2, rwkv_a0,
                                     rwkv_a2, rwkv_g2, rwkv_k_k, rwkv_k_a, rwkv_r_k, rwkv_ln_w, rwkv_ln_b)
    q, k, v = _att_qkv(pa, att_q_norm_g, att_k_norm_g)
    if k_cache is None:
        y_a = _band_attention(q, k, v, att_rel_bias)
        keep = min(ATT_CTX, T)
        k_new, v_new = k[:, T - keep:], v[:, T - keep:]
    else:
        y_a = _cached_attention(q, k, v, k_cache, v_cache, att_rel_bias)
        k_new, v_new = k, v
    mix = jnp.concatenate([y_r.astype(x.dtype), y_a.astype(x.dtype)], axis=-1)
    h = x + mix @ w_out
    f, conv_new = _conv_ffn(_rms(h, ffn_norm_g), conv_prev, ffn_w_up, ffn_conv_w, ffn_conv_b, ffn_w_down)
    return h + f, shift_new, wkv_new, k_new, v_new, conv_new


def setup_inputs(seed: int = 0) -> dict:
    key = jax.random.key(seed)
    ks = iter(jax.random.split(key, 40))
    n = lambda shape: jax.random.normal(next(ks), shape, jnp.float32)
    att_cache = min(ATT_CTX, PAST_LEN)
    L = DEPTH
    return {
        "x_prompt": n((BATCH, SEQ, D_MODEL)),
        "x_sample": n((DEC_BATCH, DEC_SEQ, D_MODEL)),
        "state_rwkv_shift": n((L, DEC_BATCH, 1, N_RC)),
        "state_rwkv_wkv": n((L, DEC_BATCH, H_RWKV, RWKV_HEAD, RWKV_HEAD)),
        "cache_attn_k": n((L, DEC_BATCH, att_cache, H_ATT, ATT_HEAD)),
        "cache_attn_v": n((L, DEC_BATCH, att_cache, H_ATT, ATT_HEAD)),
        "state_ffn_conv": n((L, DEC_BATCH, FFN_CONV - 1, D_FF)),
        "attn_norm_g": 1.0 + 0.02 * n((L, D_MODEL)),
        "w_in": n((L, D_MODEL, N_IN)) * D_MODEL ** -0.5,
        "rwkv_mu": jax.random.uniform(next(ks), (L, N_RC), jnp.float32),
        "rwkv_w0": jax.random.uniform(next(ks), (L, D_RWKV), jnp.float32, -6.0, -1.0),
        "rwkv_w2": n((L, DECAY_LORA, D_RWKV)) * DECAY_LORA ** -0.5,
        "rwkv_a0": 0.1 * n((L, D_RWKV)),
        "rwkv_a2": n((L, A_LORA, D_RWKV)) * A_LORA ** -0.5,
        "rwkv_g2": n((L, GATE_LORA, D_RWKV)) * GATE_LORA ** -0.5,
        "rwkv_k_k": 0.85 + 0.05 * n((L, D_RWKV)),
        "rwkv_k_a": 1.0 + 0.05 * n((L, D_RWKV)),
        "rwkv_r_k": 0.1 * n((L, H_RWKV, RWKV_HEAD)),
        "rwkv_ln_w": 1.0 + 0.02 * n((L, D_RWKV)),
        "rwkv_ln_b": 0.02 * n((L, D_RWKV)),
        "att_q_norm_g": 1.0 + 0.02 * n((L, ATT_HEAD)),
        "att_k_norm_g": 1.0 + 0.02 * n((L, ATT_HEAD)),
        "att_rel_bias": 0.5 * n((L, 2 * REL_CLIP + 1, H_ATT)),
        "w_out": n((L, D_MIX, D_MODEL)) * D_MIX ** -0.5,
        "ffn_norm_g": 1.0 + 0.02 * n((L, D_MODEL)),
        "ffn_w_up": n((L, D_MODEL, 2 * D_FF)) * D_MODEL ** -0.5,
        "ffn_conv_w": n((L, FFN_CONV, D_FF)) * FFN_CONV ** -0.5,
        "ffn_conv_b": 0.02 * n((L, D_FF)),
        "ffn_w_down": n((L, D_FF, D_MODEL)) * D_FF ** -0.5,
    }


def reference(x_prompt, x_sample, state_rwkv_shift, state_rwkv_wkv, cache_attn_k, cache_attn_v,
              state_ffn_conv, attn_norm_g, w_in, rwkv_mu, rwkv_w0, rwkv_w2, rwkv_a0, rwkv_a2, rwkv_g2,
              rwkv_k_k, rwkv_k_a, rwkv_r_k, rwkv_ln_w, rwkv_ln_b, att_q_norm_g, att_k_norm_g,
              att_rel_bias, w_out, ffn_norm_g, ffn_w_up, ffn_conv_w, ffn_conv_b, ffn_w_down):
    B = x_prompt.shape[0]
    xp, xs = x_prompt, x_sample
    p_states = ([], [], [], [], [])
    s_states = ([], [], [], [], [])
    for l in range(DEPTH):
        lp = (attn_norm_g[l], w_in[l], rwkv_mu[l], rwkv_w0[l], rwkv_w2[l], rwkv_a0[l], rwkv_a2[l],
              rwkv_g2[l], rwkv_k_k[l], rwkv_k_a[l], rwkv_r_k[l], rwkv_ln_w[l], rwkv_ln_b[l],
              att_q_norm_g[l], att_k_norm_g[l], att_rel_bias[l], w_out[l], ffn_norm_g[l],
              ffn_w_up[l], ffn_conv_w[l], ffn_conv_b[l], ffn_w_down[l])
        shift0 = jnp.zeros((B, 1, N_RC), xp.dtype)
        wkv0 = jnp.zeros((B, H_RWKV, RWKV_HEAD, RWKV_HEAD), jnp.float32)
        conv0 = jnp.zeros((B, FFN_CONV - 1, D_FF), xp.dtype)
        xp, *ps = _block(xp, shift0, wkv0, conv0, None, None, lp)
        xs, *ss = _block(xs, state_rwkv_shift[l], state_rwkv_wkv[l], state_ffn_conv[l],
                         cache_attn_k[l], cache_attn_v[l], lp)
        for i in range(5):
            p_states[i].append(ps[i])
            s_states[i].append(ss[i])
    p_shift, p_wkv, p_k, p_v, p_conv = [jnp.stack(t, axis=0) for t in p_states]
    s_shift, s_wkv, s_k, s_v, s_conv = [jnp.stack(t, axis=0) for t in s_states]
    y_prompt, y_sample = xp, xs
    return (y_prompt, y_sample, p_shift, p_wkv, p_k, p_v, p_conv, s_shift, s_wkv, s_k, s_v, s_conv)
```

```python
import functools

import jax
import jax.numpy as jnp
from jax import lax
from jax.experimental import pallas as pl
from jax.experimental.pallas import tpu as pltpu

F32 = jnp.float32
BF16 = jnp.bfloat16

RMS_EPS = 1e-6
GN_EPS = 64e-5
NEG_INF = -1e30
CHUNK = 64
N_PAST_CHUNKS = 8
ATT_CTX = N_PAST_CHUNKS * CHUNK
REL_CLIP = 128
RWKV_HEAD = 64
ATT_HEAD = 128
FFN_CONV = 3

LANES = 128
SUBLANES = 8
VMEM_LIMIT = 56 * 1024 * 1024
TOEP_W = 2048


def _cparams(*sem):
    return pltpu.CompilerParams(dimension_semantics=sem, vmem_limit_bytes=VMEM_LIMIT)


def _round_up(n, m):
    return (n + m - 1) // m * m


def _pick_tile(n, candidates):
    for c in candidates:
        if n % c == 0:
            return c
    raise ValueError(f"no tile in {candidates} divides {n}")


def _pair_ones():
    r = lax.broadcasted_iota(jnp.int32, (LANES, LANES), 0) // RWKV_HEAD
    c = lax.broadcasted_iota(jnp.int32, (LANES, LANES), 1) // RWKV_HEAD
    return (r == c).astype(BF16)


def _seg_sum(x, ones):
    hi = x.astype(BF16)
    lo = (x - hi.astype(F32)).astype(BF16)
    return (jnp.dot(hi, ones, preferred_element_type=F32)
            + jnp.dot(lo, ones, preferred_element_type=F32))


def _softplus(z):
    return jnp.maximum(z, 0.0) + jnp.log1p(jnp.exp(-jnp.abs(z)))


def _sigmoid(z):
    return 1.0 / (1.0 + jnp.exp(-z))


def _rms_matmul_kernel(x_ref, g_ref, w_ref, o_ref, xn_ref):
    @pl.when(pl.program_id(1) == 0)
    def _():
        x = x_ref[...]
        ms = jnp.mean(x * x, axis=-1, keepdims=True)
        xn_ref[...] = (x * lax.rsqrt(ms + RMS_EPS) * g_ref[...]).astype(BF16)

    o_ref[...] = jnp.dot(xn_ref[...], w_ref[...], preferred_element_type=F32)


def _rms_matmul(x, g, w):
    M, D = x.shape
    N = w.shape[1]
    tm = _pick_tile(M, (512, 256, 128, 64, 32))
    tn = _pick_tile(N, (512, 256, 128))
    return pl.pallas_call(
        _rms_matmul_kernel,
        out_shape=jax.ShapeDtypeStruct((M, N), F32),
        grid=(M // tm, N // tn),
        in_specs=[
            pl.BlockSpec((tm, D), lambda m, n: (m, 0)),
            pl.BlockSpec((1, D), lambda m, n: (0, 0)),
            pl.BlockSpec((D, tn), lambda m, n: (0, n)),
        ],
        out_specs=pl.BlockSpec((tm, tn), lambda m, n: (m, n)),
        scratch_shapes=[pltpu.VMEM((tm, D), BF16)],
        compiler_params=_cparams("parallel", "arbitrary"),
        name="rms_inproj",
    )(x, g.reshape(1, D), w)


def _rwkv_prep_kernel(p_ref, prev_ref, mu_ref, w0_ref, a0_ref, kk_ref, ka_ref, rk_ref,
                      w2_ref, a2_ref, g2_ref,
                      a_out, b_out, w_out, k_out, rp_out, v_out, g_out, kr_out, bon_out,
                      last_ref, *, dr):
    @pl.when(pl.program_id(1) == 0)
    def _():
        last_ref[...] = prev_ref[...]

    p = p_ref[...]
    tb = p.shape[0]
    row = lax.broadcasted_iota(jnp.int32, p.shape, 0)
    shifted = jnp.where(row == 0, last_ref[...], pltpu.roll(p, 1, axis=0))
    last_ref[...] = p[tb - 1:tb, :]
    xm = p + (shifted - p) * mu_ref[...]

    lo = xm[:, 3 * dr:]
    wl = jnp.dot(jnp.tanh(lo).astype(BF16), w2_ref[...], preferred_element_type=F32)
    al = jnp.dot(lo.astype(BF16), a2_ref[...], preferred_element_type=F32)
    g_out[...] = jnp.dot(_sigmoid(lo).astype(BF16), g2_ref[...], preferred_element_type=F32)

    ones = _pair_ones()
    for c in range(dr // LANES):
        sl = slice(c * LANES, (c + 1) * LANES)
        r = xm[:, c * LANES:(c + 1) * LANES]
        k = xm[:, dr + c * LANES:dr + (c + 1) * LANES]
        v = xm[:, 2 * dr + c * LANES:2 * dr + (c + 1) * LANES]
        decay = jnp.exp(-jnp.exp(-_softplus(-(w0_ref[:, sl] + wl[:, sl])) - 0.5))
        asig = _sigmoid(a0_ref[:, sl] + al[:, sl])
        kk = k * kk_ref[:, sl]
        kk = kk * lax.rsqrt(jnp.maximum(_seg_sum(kk * kk, ones), 1e-24))
        k2 = k * (1.0 + (asig - 1.0) * ka_ref[:, sl])
        a_s = -kk
        b_s = kk * asig
        rp = decay * r + a_s * _seg_sum(b_s * r, ones)
        rk2 = r * k2
        a_out[:, sl] = a_s
        b_out[:, sl] = b_s
        w_out[:, sl] = decay
        k_out[:, sl] = k2
        rp_out[:, sl] = rp
        v_out[:, sl] = v
        kr_out[:, sl] = _seg_sum(rk2, ones)
        bon_out[:, sl] = _seg_sum(rk2 * rk_ref[:, sl], ones)


def _rwkv_prep(proj, shift_prev, nseq, tseq, cr, dr, mu_p, w0, a0, k_k, k_a, r_k, w2p, a2p, g2p):
    M = proj.shape[0]
    tb = _pick_tile(tseq, (128, 64, 32))
    nt = tseq // tb
    lp = cr - 3 * dr
    row = lambda a: a.reshape(1, -1).astype(F32)
    vec_spec = pl.BlockSpec((1, dr), lambda b, i: (0, 0))
    mat_spec = pl.BlockSpec((lp, dr), lambda b, i: (0, 0))
    out_spec = pl.BlockSpec((tb, dr), lambda b, i: (b * nt + i, 0))
    outs = pl.pallas_call(
        functools.partial(_rwkv_prep_kernel, dr=dr),
        out_shape=[jax.ShapeDtypeStruct((M, dr), F32)] * 9,
        grid=(nseq, nt),
        in_specs=[
            pl.BlockSpec((tb, cr), lambda b, i: (b * nt + i, 0)),
            pl.BlockSpec((None, 1, cr), lambda b, i: (b, 0, 0)),
            pl.BlockSpec((1, cr), lambda b, i: (0, 0)),
            vec_spec, vec_spec, vec_spec, vec_spec, vec_spec,
            mat_spec, mat_spec, mat_spec,
        ],
        out_specs=[out_spec] * 9,
        scratch_shapes=[pltpu.VMEM((1, cr), F32)],
        compiler_params=_cparams("parallel", "arbitrary"),
        name="rwkv_prep",
    )(proj, shift_prev, mu_p, row(w0), row(a0), row(k_k), row(k_a), row(r_k), w2p, a2p, g2p)
    return outs


def _pair_transpose(x):
    t2 = jnp.concatenate([x, x], axis=0).T
    lane = lax.broadcasted_iota(jnp.int32, x.shape, 1)
    return jnp.where(lane < RWKV_HEAD, t2[:RWKV_HEAD], t2[RWKV_HEAD:])


def _wkv_kernel(a_ref, b_ref, w_ref, k_ref, rp_ref, v_ref, g_ref, kr_ref, bon_ref,
                s0_ref, lnw_ref, lnb_ref, y_ref, s_out_ref, s_scr, vt_scr, yt_scr, *, tb, npairs):
    i = pl.program_id(1)

    @pl.when(i == 0)
    def _():
        s_scr[...] = s0_ref[...]

    ones = _pair_ones()
    n = RWKV_HEAD
    for p in range(npairs):
        v = v_ref[:, p * LANES:(p + 1) * LANES]
        if tb < n:
            v = jnp.concatenate([v, jnp.zeros((n - tb, LANES), F32)], axis=0)
        vt_scr[p] = _pair_transpose(v)
        yt_scr[p] = jnp.zeros((n, LANES), F32)

    lane = lax.broadcasted_iota(jnp.int32, (n, LANES), 1) % n

    def steps(t8, carry):
        rows = pl.ds(pl.multiple_of(t8 * SUBLANES, SUBLANES), SUBLANES)
        for p in range(npairs):
            sl = slice(p * LANES, (p + 1) * LANES)
            a8, rp8, w8, b8, k8 = (r[rows, sl] for r in (a_ref, rp_ref, w_ref, b_ref, k_ref))
            s = s_scr[p]
            vt = vt_scr[p]
            yt = yt_scr[p]
            for j in range(SUBLANES):
                mask = lane == t8 * SUBLANES + j
                lhs = jnp.concatenate([
                    (s * a8[j:j + 1]).astype(BF16),
                    (s * rp8[j:j + 1]).astype(BF16),
                    jnp.where(mask, vt, 0.0).astype(BF16)], axis=0)
                res = jnp.dot(lhs, ones, preferred_element_type=F32)
                sa, y, vb = res[:n], res[n:2 * n], res[2 * n:]
                s = s * w8[j:j + 1] + sa * b8[j:j + 1] + vb * k8[j:j + 1]
                yt = jnp.where(mask, y, yt)
            s_scr[p] = s
            yt_scr[p] = yt
        return carry

    lax.fori_loop(0, tb // SUBLANES, steps, 0)

    inv_n = 1.0 / n
    for p in range(npairs):
        sl = slice(p * LANES, (p + 1) * LANES)
        v = v_ref[:, sl]
        y = _pair_transpose(yt_scr[p])[:tb] + v * kr_ref[:, sl]
        mean = _seg_sum(y, ones) * inv_n
        d = y - mean
        var = _seg_sum(d * d, ones) * inv_n
        yn = d * lax.rsqrt(var + GN_EPS) * lnw_ref[:, sl] + lnb_ref[:, sl]
        y_ref[:, sl] = ((yn + bon_ref[:, sl] * v) * g_ref[:, sl]).astype(BF16)

    @pl.when(i == pl.num_programs(1) - 1)
    def _():
        s_out_ref[...] = s_scr[...]


def _wkv(prep, s0_pairs, nseq, tseq, dr, ln_w, ln_b):
    M = prep[0].shape[0]
    tb = min(tseq, RWKV_HEAD)
    nt = tseq // tb
    npairs = dr // LANES
    blk = pl.BlockSpec((tb, dr), lambda b, i: (b * nt + i, 0))
    st = pl.BlockSpec((None, npairs, RWKV_HEAD, LANES), lambda b, i: (b, 0, 0, 0))
    vec = pl.BlockSpec((1, dr), lambda b, i: (0, 0))
    return pl.pallas_call(
        functools.partial(_wkv_kernel, tb=tb, npairs=npairs),
        out_shape=[jax.ShapeDtypeStruct((M, dr), BF16),
                   jax.ShapeDtypeStruct(s0_pairs.shape, F32)],
        grid=(nseq, nt),
        in_specs=[blk] * 9 + [st, vec, vec],
        out_specs=[blk, st],
        scratch_shapes=[pltpu.VMEM((npairs, RWKV_HEAD, LANES), F32)] * 3,
        compiler_params=_cparams("parallel", "arbitrary"),
        name="wkv7",
    )(*prep, s0_pairs, ln_w.reshape(1, dr).astype(F32), ln_b.reshape(1, dr).astype(F32))


def _state_to_pairs(s):
    B, H, n, _ = s.shape
    return s.reshape(B, H // 2, 2, n, n).transpose(0, 1, 3, 2, 4).reshape(B, H // 2, n, 2 * n)


def _pairs_to_state(s):
    B, hp, n, _ = s.shape
    return s.reshape(B, hp, n, 2, n).transpose(0, 1, 3, 2, 4).reshape(B, 2 * hp, n, n)


def _bias_kernel(tab_ref, band_ref, dec_ref, *, nrel, qb, dec_t, dec_w):
    h = pl.program_id(0)
    m = lax.broadcasted_iota(jnp.int32, (SUBLANES, TOEP_W), 1)
    dd = jnp.where(m < TOEP_W // 2, m, m - TOEP_W)
    idx = jnp.clip(dd - ATT_CTX, -REL_CLIP, REL_CLIP) + REL_CLIP

    def pick(r, acc):
        return jnp.where(idx == r, tab_ref[h * nrel + r], acc)

    e = lax.fori_loop(0, nrel, pick, jnp.zeros((SUBLANES, TOEP_W), F32))
    eb = jnp.broadcast_to(e[0:1], (qb, TOEP_W))
    toe = pltpu.roll(eb, 0, axis=1, stride=1, stride_axis=0)
    kb = band_ref.shape[-1]
    cq = lax.broadcasted_iota(jnp.int32, (qb, kb), 0) // CHUNK
    ck = lax.broadcasted_iota(jnp.int32, (qb, kb), 1) // CHUNK
    valid = (ck >= cq) & (ck <= cq + N_PAST_CHUNKS)
    band_ref[...] = jnp.where(valid, toe[:, :kb], NEG_INF)
    dec_ref[...] = toe[:dec_t, :dec_w]


def _bias_tiles(table, qb, dec_t, dec_w):
    nrel, H = table.shape
    tab = table.T.reshape(-1).astype(F32)
    return pl.pallas_call(
        functools.partial(_bias_kernel, nrel=nrel, qb=qb, dec_t=dec_t, dec_w=dec_w),
        out_shape=[jax.ShapeDtypeStruct((H, qb, qb + ATT_CTX), F32),
                   jax.ShapeDtypeStruct((H, dec_t, dec_w), F32)],
        grid=(H,),
        in_specs=[pl.BlockSpec(memory_space=pltpu.SMEM)],
        out_specs=[pl.BlockSpec((None, qb, qb + ATT_CTX), lambda h: (h, 0, 0)),
                   pl.BlockSpec((None, dec_t, dec_w), lambda h: (h, 0, 0))],
        compiler_params=_cparams("arbitrary"),
        name="rel_bias_tiles",
    )(tab)


def _head_rms(x, g):
    return x * lax.rsqrt(jnp.mean(x * x, axis=-1, keepdims=True) + RMS_EPS) * g


def _dot_nt(a, b):
    return lax.dot_general(a, b, (((1,), (1,)), ((), ())), preferred_element_type=F32)


def _band_attn_kernel(q_ref, kc_ref, kp_ref, vc_ref, vp_ref, qg_ref, kg_ref, bias_ref,
                      o_ref, knew_ref, *, qb):
    i = pl.program_id(2)
    scale = ATT_HEAD ** -0.5
    qn = _head_rms(q_ref[...], qg_ref[...]).astype(BF16)
    kc = _head_rms(kc_ref[...], kg_ref[...])
    kp = _head_rms(kp_ref[...], kg_ref[...])
    s_p = _dot_nt(qn, kp.astype(BF16)) * scale + bias_ref[:, :ATT_CTX]
    s_p = jnp.where(i > 0, s_p, NEG_INF)
    s_c = _dot_nt(qn, kc.astype(BF16)) * scale + bias_ref[:, ATT_CTX:]
    m = jnp.maximum(jnp.max(s_p, axis=-1, keepdims=True), jnp.max(s_c, axis=-1, keepdims=True))
    p_p = jnp.exp(s_p - m)
    p_c = jnp.exp(s_c - m)
    l = jnp.sum(p_p, axis=-1, keepdims=True) + jnp.sum(p_c, axis=-1, keepdims=True)
    o = (jnp.dot(p_p.astype(BF16), vp_ref[...].astype(BF16), preferred_element_type=F32)
         + jnp.dot(p_c.astype(BF16), vc_ref[...].astype(BF16), preferred_element_type=F32))
    o_ref[...] = (o / l).astype(BF16)

    @pl.when(i == pl.num_programs(2) - 1)
    def _():
        knew_ref[...] = kc


def _band_attention(proj, col0, nseq, tseq, nheads, q_g, k_g, bias_band):
    M = proj.shape[0]
    qb = ATT_CTX
    assert tseq % qb == 0
    nb = tseq // qb
    c0 = col0 // ATT_HEAD
    H = nheads
    cur = lambda off: pl.BlockSpec((qb, ATT_HEAD), lambda b, h, i: (b * nb + i, c0 + off + h))
    prv = lambda off: pl.BlockSpec(
        (qb, ATT_HEAD), lambda b, h, i: (b * nb + jnp.maximum(i - 1, 0), c0 + off + h))
    gspec = pl.BlockSpec((1, ATT_HEAD), lambda b, h, i: (0, 0))
    return pl.pallas_call(
        functools.partial(_band_attn_kernel, qb=qb),
        out_shape=[jax.ShapeDtypeStruct((M, H * ATT_HEAD), BF16),
                   jax.ShapeDtypeStruct((nseq, qb, H * ATT_HEAD), F32)],
        grid=(nseq, H, nb),
        in_specs=[cur(0), cur(H), prv(H), cur(2 * H), prv(2 * H), gspec, gspec,
                  pl.BlockSpec((None, qb, qb + ATT_CTX), lambda b, h, i: (h, 0, 0))],
        out_specs=[pl.BlockSpec((qb, ATT_HEAD), lambda b, h, i: (b * nb + i, h)),
                   pl.BlockSpec((None, qb, ATT_HEAD), lambda b, h, i: (b, 0, h))],
        compiler_params=_cparams("parallel", "parallel", "arbitrary"),
        name="band_attention",
    )(proj, proj, proj, proj, proj, q_g.reshape(1, -1), k_g.reshape(1, -1), bias_band)


def _cached_attn_kernel(q_ref, kn_ref, vn_ref, kc_ref, vc_ref, qg_ref, kg_ref, bias_ref,
                        o_ref, knew_ref, *, t):
    scale = ATT_HEAD ** -0.5
    L = kc_ref.shape[0]
    qn = _head_rms(q_ref[...], qg_ref[...]).astype(BF16)
    kn = _head_rms(kn_ref[...], kg_ref[...])
    knew_ref[...] = kn
    s_c = _dot_nt(qn, kc_ref[...].astype(BF16)) * scale + bias_ref[:, :L]
    s_n = _dot_nt(qn, kn.astype(BF16)) * scale + bias_ref[:, L:L + t]
    m = jnp.maximum(jnp.max(s_c, axis=-1, keepdims=True), jnp.max(s_n, axis=-1, keepdims=True))
    p_c = jnp.exp(s_c - m)
    p_n = jnp.exp(s_n - m)
    l = jnp.sum(p_c, axis=-1, keepdims=True) + jnp.sum(p_n, axis=-1, keepdims=True)
    o = (jnp.dot(p_c.astype(BF16), vc_ref[...].astype(BF16), preferred_element_type=F32)
         + jnp.dot(p_n.astype(BF16), vn_ref[...].astype(BF16), preferred_element_type=F32))
    o_ref[...] = (o / l).astype(BF16)


def _cached_attention(proj, col0, nseq, tseq, nheads, k_cache, v_cache, q_g, k_g, bias_dec):
    M = proj.shape[0]
    H = nheads
    L = k_cache.shape[1]
    c0 = col0 // ATT_HEAD
    new = lambda off: pl.BlockSpec((tseq, ATT_HEAD), lambda b, h: (b, c0 + off + h))
    cache = pl.BlockSpec((None, L, ATT_HEAD), lambda b, h: (b, 0, h))
    gspec = pl.BlockSpec((1, ATT_HEAD), lambda b, h: (0, 0))
    out = pl.BlockSpec((tseq, ATT_HEAD), lambda b, h: (b, h))
    return pl.pallas_call(
        functools.partial(_cached_attn_kernel, t=tseq),
        out_shape=[jax.ShapeDtypeStruct((M, H * ATT_HEAD), BF16),
                   jax.ShapeDtypeStruct((M, H * ATT_HEAD), F32)],
        grid=(nseq, H),
        in_specs=[new(0), new(H), new(2 * H), cache, cache, gspec, gspec,
                  pl.BlockSpec((None,) + bias_dec.shape[1:], lambda b, h: (h, 0, 0))],
        out_specs=[out, out],
        compiler_params=_cparams("parallel", "arbitrary"),
        name="cached_attention",
    )(proj, proj, proj, k_cache, v_cache, q_g.reshape(1, -1), k_g.reshape(1, -1), bias_dec)


def _outproj_kernel(yr_ref, ya_ref, w_ref, x_ref, h_ref, *, dr):
    acc = jnp.dot(yr_ref[...], w_ref[:dr, :], preferred_element_type=F32)
    acc += jnp.dot(ya_ref[...], w_ref[dr:, :], preferred_element_type=F32)
    h_ref[...] = x_ref[...] + acc


def _outproj(y_r, y_a, w_out, x):
    M, D = x.shape
    dr, da = y_r.shape[1], y_a.shape[1]
    tm = _pick_tile(M, (512, 256, 128, 64, 32))
    tn = _pick_tile(D, (512, 256, 128))
    return pl.pallas_call(
        functools.partial(_outproj_kernel, dr=dr),
        out_shape=jax.ShapeDtypeStruct((M, D), F32),
        grid=(M // tm, D // tn),
        in_specs=[
            pl.BlockSpec((tm, dr), lambda m, n: (m, 0)),
            pl.BlockSpec((tm, da), lambda m, n: (m, 0)),
            pl.BlockSpec((dr + da, tn), lambda m, n: (0, n)),
            pl.BlockSpec((tm, tn), lambda m, n: (m, n)),
        ],
        out_specs=pl.BlockSpec((tm, tn), lambda m, n: (m, n)),
        compiler_params=_cparams("parallel", "arbitrary"),
        name="out_proj",
    )(y_r, y_a, w_out, x)


def _ffn_up_kernel(h_ref, g_ref, wg_ref, wv_ref, cw_ref, cb_ref, b1_ref, b2_ref,
                   act_ref, gate_ref, hn_ref, carry_ref, *, tseq, blocks_per_seq):
    m = pl.program_id(0)
    n = pl.program_id(1)

    @pl.when(n == 0)
    def _():
        x = h_ref[...]
        ms = jnp.mean(x * x, axis=-1, keepdims=True)
        hn_ref[...] = (x * lax.rsqrt(ms + RMS_EPS) * g_ref[...]).astype(BF16)

    hn = hn_ref[...]
    gate = jnp.dot(hn, wg_ref[...], preferred_element_type=F32)
    val = jnp.dot(hn, wv_ref[...], preferred_element_type=F32)
    tm = gate.shape[0]
    row = lax.broadcasted_iota(jnp.int32, gate.shape, 0)
    r1 = pltpu.roll(gate, 1, axis=0)
    r2 = pltpu.roll(gate, 2, axis=0)
    if blocks_per_seq >= 1:
        first = m % blocks_per_seq == 0
        prev = carry_ref[n]
        p2 = jnp.where(first, b2_ref[0:1, :], prev[SUBLANES - 2:SUBLANES - 1, :])
        p1 = jnp.where(first, b1_ref[0:1, :], prev[SUBLANES - 1:SUBLANES, :])
        g1 = jnp.where(row == 0, p1, r1)
        g2 = jnp.where(row == 0, p2, jnp.where(row == 1, p1, r2))
        carry_ref[n] = gate[tm - SUBLANES:, :]
        gate_ref[...] = gate[tm - (FFN_CONV - 1):, :]
    else:
        t = row % tseq
        g1 = jnp.where(t == 0, b1_ref[...], r1)
        g2 = jnp.where(t < 2, b2_ref[...], r2)
        gate_ref[...] = gate
    conv = cb_ref[...] + g2 * cw_ref[0:1, :] + g1 * cw_ref[1:2, :] + gate * cw_ref[2:3, :]
    act_ref[...] = (conv * _sigmoid(conv) * val).astype(BF16)


def _ffn_up(h, g, w_up, conv_w, conv_b, conv_prev, nseq, tseq):
    M, D = h.shape
    Fd = w_up.shape[1] // 2
    tn = _pick_tile(Fd, (256, 128))
    nn = Fd // tn
    if tseq >= 512:
        tm = 512
        bps = tseq // tm
        b1 = conv_prev[:, 1:2, :]
        b2 = conv_prev[:, 0:1, :]
        bspec = pl.BlockSpec((None, 1, tn), lambda m, n: (m // bps, 0, n))
        gate_shape = jax.ShapeDtypeStruct((M // tm, FFN_CONV - 1, Fd), F32)
        gate_spec = pl.BlockSpec((None, FFN_CONV - 1, tn), lambda m, n: (m, 0, n))
    else:
        tm = M
        bps = 0
        z = jnp.zeros((nseq, tseq, Fd), F32)
        b1 = z.at[:, 0].set(conv_prev[:, 1]).reshape(M, Fd)
        b2 = z.at[:, 0].set(conv_prev[:, 0]).at[:, 1].set(conv_prev[:, 1]).reshape(M, Fd)
        bspec = pl.BlockSpec((tm, tn), lambda m, n: (m, n))
        gate_shape = jax.ShapeDtypeStruct((M, Fd), F32)
        gate_spec = pl.BlockSpec((tm, tn), lambda m, n: (m, n))
    act, gate = pl.pallas_call(
        functools.partial(_ffn_up_kernel, tseq=tseq, blocks_per_seq=bps),
        out_shape=[jax.ShapeDtypeStruct((M, Fd), BF16), gate_shape],
        grid=(M // tm, nn),
        in_specs=[
            pl.BlockSpec((tm, D), lambda m, n: (m, 0)),
            pl.BlockSpec((1, D), lambda m, n: (0, 0)),
            pl.BlockSpec((D, tn), lambda m, n: (0, n)),
            pl.BlockSpec((D, tn), lambda m, n: (0, nn + n)),
            pl.BlockSpec((FFN_CONV, tn), lambda m, n: (0, n)),
            pl.BlockSpec((1, tn), lambda m, n: (0, n)),
            bspec, bspec,
        ],
        out_specs=[pl.BlockSpec((tm, tn), lambda m, n: (m, n)), gate_spec],
        scratch_shapes=[pltpu.VMEM((tm, D), BF16), pltpu.VMEM((nn, SUBLANES, tn), F32)],
        compiler_params=_cparams("arbitrary", "arbitrary"),
        name="ffn_up",
    )(h, g.reshape(1, D), w_up, w_up, conv_w, conv_b.reshape(1, Fd), b1, b2)
    if bps == 0:
        gate = gate.reshape(nseq, tseq, Fd)[:, tseq - (FFN_CONV - 1):]
    else:
        gate = gate[bps - 1::bps]
    return act, gate


def _ffn_down_kernel(a_ref, w_ref, h_ref, o_ref):
    o_ref[...] = h_ref[...] + jnp.dot(a_ref[...], w_ref[...], preferred_element_type=F32)


def _ffn_down(act, w_down, h):
    M, Fd = act.shape
    D = w_down.shape[1]
    tm = _pick_tile(M, (512, 256, 128, 64, 32))
    tn = _pick_tile(D, (256, 128))
    return pl.pallas_call(
        _ffn_down_kernel,
        out_shape=jax.ShapeDtypeStruct((M, D), F32),
        grid=(M // tm, D // tn),
        in_specs=[
            pl.BlockSpec((tm, Fd), lambda m, n: (m, 0)),
            pl.BlockSpec((Fd, tn), lambda m, n: (0, n)),
            pl.BlockSpec((tm, tn), lambda m, n: (m, n)),
        ],
        out_specs=pl.BlockSpec((tm, tn), lambda m, n: (m, n)),
        compiler_params=_cparams("parallel", "arbitrary"),
        name="ffn_down",
    )(act, w_down, h)


def _layer_weights(lp):
    (attn_norm_g, w_in, mu, w0, w2, a0, a2, g2, k_k, k_a, r_k, ln_w, ln_b, q_g, k_g, rel_bias,
     w_out, ffn_norm_g, w_up, conv_w, conv_b, w_down) = lp
    D = w_in.shape[0]
    dr = w0.shape[0]
    dl, al, gl = w2.shape[0], a2.shape[0], g2.shape[0]
    nrc = 3 * dr + dl + al + gl
    lpad = _round_up(dl + al + gl, LANES)
    cr = 3 * dr + lpad
    zc = jnp.zeros((D, cr - nrc), w_in.dtype)
    w_in_p = jnp.concatenate([w_in[:, :nrc], zc, w_in[:, nrc:]], axis=1).astype(BF16)
    mu_p = jnp.concatenate([mu, jnp.zeros((cr - nrc,), mu.dtype)]).reshape(1, cr)
    zrow = lambda n: jnp.zeros((n, dr), F32)
    w2p = jnp.concatenate([w2, zrow(lpad - dl)], axis=0).astype(BF16)
    a2p = jnp.concatenate([zrow(dl), a2, zrow(lpad - dl - al)], axis=0).astype(BF16)
    g2p = jnp.concatenate([zrow(dl + al), g2, zrow(lpad - dl - al - gl)], axis=0).astype(BF16)
    return dict(attn_norm_g=attn_norm_g, w_in_p=w_in_p, mu_p=mu_p, w0=w0, a0=a0, k_k=k_k, k_a=k_a,
                r_k=r_k.reshape(-1), w2p=w2p, a2p=a2p, g2p=g2p, ln_w=ln_w, ln_b=ln_b, q_g=q_g, k_g=k_g,
                rel_bias=rel_bias, w_out=w_out.astype(BF16), ffn_norm_g=ffn_norm_g,
                w_up=w_up.astype(BF16), conv_w=conv_w, conv_b=conv_b, w_down=w_down.astype(BF16),
                nrc=nrc, cr=cr, dr=dr)


def _block(x, shift_prev, wkv_prev, conv_prev, k_cache, v_cache, W, bias_band, bias_dec):
    B, T, D = x.shape
    M = B * T
    dr, cr, nrc = W["dr"], W["cr"], W["nrc"]
    H = (D - dr) // ATT_HEAD
    xf = x.reshape(M, D)
    proj = _rms_matmul(xf, W["attn_norm_g"], W["w_in_p"])

    shift_p = jnp.concatenate([shift_prev, jnp.zeros((B, 1, cr - nrc), F32)], axis=-1)
    prep = _rwkv_prep(proj, shift_p, B, T, cr, dr, W["mu_p"], W["w0"], W["a0"], W["k_k"], W["k_a"],
                      W["r_k"], W["w2p"], W["a2p"], W["g2p"])
    y_r, s_pairs = _wkv(prep, _state_to_pairs(wkv_prev), B, T, dr, W["ln_w"], W["ln_b"])
    wkv_new = _pairs_to_state(s_pairs)
    shift_new = proj.reshape(B, T, -1)[:, T - 1:, :nrc]

    da = H * ATT_HEAD
    if k_cache is None:
        y_a, k_new = _band_attention(proj, cr, B, T, H, W["q_g"], W["k_g"], bias_band)
        keep = min(ATT_CTX, T)
        k_new = k_new.reshape(B, keep, H, ATT_HEAD)
        v_new = proj.reshape(B, T, -1)[:, T - keep:, cr + 2 * da:].reshape(B, keep, H, ATT_HEAD)
    else:
        L = k_cache.shape[1]
        y_a, k_new = _cached_attention(proj, cr, B, T, H, k_cache.reshape(B, L, da),
                                       v_cache.reshape(B, L, da), W["q_g"], W["k_g"], bias_dec)
        k_new = k_new.reshape(B, T, H, ATT_HEAD)
        v_new = proj.reshape(B, T, -1)[:, :, cr + 2 * da:].reshape(B, T, H, ATT_HEAD)

    h = _outproj(y_r, y_a, W["w_out"], xf)
    act, conv_new = _ffn_up(h, W["ffn_norm_g"], W["w_up"], W["conv_w"], W["conv_b"], conv_prev, B, T)
    y = _ffn_down(act, W["w_down"], h)
    return y.reshape(B, T, D), shift_new, wkv_new, k_new, v_new, conv_new


def kernel(x_prompt, x_sample, state_rwkv_shift, state_rwkv_wkv, cache_attn_k, cache_attn_v,
           state_ffn_conv, attn_norm_g, w_in, rwkv_mu, rwkv_w0, rwkv_w2, rwkv_a0, rwkv_a2, rwkv_g2,
           rwkv_k_k, rwkv_k_a, rwkv_r_k, rwkv_ln_w, rwkv_ln_b, att_q_norm_g, att_k_norm_g,
           att_rel_bias, w_out, ffn_norm_g, ffn_w_up, ffn_conv_w, ffn_conv_b, ffn_w_down):
    depth = w_in.shape[0]
    B = x_prompt.shape[0]
    xp, xs = x_prompt, x_sample
    dec_t = x_sample.shape[1]
    L = cache_attn_k.shape[2]
    assert L == ATT_CTX
    dec_w = _round_up(L + dec_t, LANES)
    p_states = ([], [], [], [], [])
    s_states = ([], [], [], [], [])
    for l in range(depth):
        lp = (attn_norm_g[l], w_in[l], rwkv_mu[l], rwkv_w0[l], rwkv_w2[l], rwkv_a0[l], rwkv_a2[l],
              rwkv_g2[l], rwkv_k_k[l], rwkv_k_a[l], rwkv_r_k[l], rwkv_ln_w[l], rwkv_ln_b[l],
              att_q_norm_g[l], att_k_norm_g[l], att_rel_bias[l], w_out[l], ffn_norm_g[l],
              ffn_w_up[l], ffn_conv_w[l], ffn_conv_b[l], ffn_w_down[l])
        W = _layer_weights(lp)
        nrc, dr = W["nrc"], W["dr"]
        Fd = ffn_conv_b.shape[-1]
        bias_band, bias_dec = _bias_tiles(att_rel_bias[l], ATT_CTX, dec_t, dec_w)
        shift0 = jnp.zeros((B, 1, nrc), F32)
        wkv0 = jnp.zeros((B, dr // RWKV_HEAD, RWKV_HEAD, RWKV_HEAD), F32)
        conv0 = jnp.zeros((B, FFN_CONV - 1, Fd), F32)
        xp, *ps = _block(xp, shift0, wkv0, conv0, None, None, W, bias_band, bias_dec)
        xs, *ss = _block(xs, state_rwkv_shift[l], state_rwkv_wkv[l], state_ffn_conv[l],
                         cache_attn_k[l], cache_attn_v[l], W, bias_band, bias_dec)
        for i in range(5):
            p_states[i].append(ps[i])
            s_states[i].append(ss[i])
    p_out = [jnp.stack(t, axis=0) for t in p_states]
    s_out = [jnp.stack(t, axis=0) for t in s_states]
    return (xp, xs, *p_out, *s_out)
```

```python
import functools

import jax
import jax.numpy as jnp
from jax import lax
from jax.experimental import pallas as pl
from jax.experimental.pallas import tpu as pltpu

F32 = jnp.float32
BF16 = jnp.bfloat16

RMS_EPS = 1e-6
GN_EPS = 64e-5
NEG_INF = -1e30
CHUNK = 64
N_PAST_CHUNKS = 8
ATT_CTX = N_PAST_CHUNKS * CHUNK
REL_CLIP = 128
RWKV_HEAD = 64
ATT_HEAD = 128
FFN_CONV = 3

LANES = 128
SUBLANES = 8
VMEM_LIMIT = 56 * 1024 * 1024
TOEP_W = 2048


def _cparams(*sem):
    return pltpu.CompilerParams(dimension_semantics=sem, vmem_limit_bytes=VMEM_LIMIT)


def _round_up(n, m):
    return (n + m - 1) // m * m


def _pick_tile(n, candidates):
    for c in candidates:
        if n % c == 0:
            return c
    raise ValueError(f"no tile in {candidates} divides {n}")


def _pair_ones():
    r = lax.broadcasted_iota(jnp.int32, (LANES, LANES), 0) // RWKV_HEAD
    c = lax.broadcasted_iota(jnp.int32, (LANES, LANES), 1) // RWKV_HEAD
    return (r == c).astype(BF16)


def _seg_sum(x, ones):
    hi = x.astype(BF16)
    lo = (x - hi.astype(F32)).astype(BF16)
    return (jnp.dot(hi, ones, preferred_element_type=F32)
            + jnp.dot(lo, ones, preferred_element_type=F32))


def _softplus(z):
    return jnp.maximum(z, 0.0) + jnp.log1p(jnp.exp(-jnp.abs(z)))


def _sigmoid(z):
    return 1.0 / (1.0 + jnp.exp(-z))


def _rms_matmul_kernel(x_ref, g_ref, w_ref, o_ref, xn_ref):
    @pl.when(pl.program_id(1) == 0)
    def _():
        x = x_ref[...]
        ms = jnp.mean(x * x, axis=-1, keepdims=True)
        xn_ref[...] = (x * lax.rsqrt(ms + RMS_EPS) * g_ref[...]).astype(BF16)

    o_ref[...] = jnp.dot(xn_ref[...], w_ref[...], preferred_element_type=F32)


def _rms_matmul(x, g, w):
    M, D = x.shape
    N = w.shape[1]
    tm = _pick_tile(M, (512, 256, 128, 64, 32))
    tn = _pick_tile(N, (512, 256, 128))
    return pl.pallas_call(
        _rms_matmul_kernel,
        out_shape=jax.ShapeDtypeStruct((M, N), F32),
        grid=(M // tm, N // tn),
        in_specs=[
            pl.BlockSpec((tm, D), lambda m, n: (m, 0)),
            pl.BlockSpec((1, D), lambda m, n: (0, 0)),
            pl.BlockSpec((D, tn), lambda m, n: (0, n)),
        ],
        out_specs=pl.BlockSpec((tm, tn), lambda m, n: (m, n)),
        scratch_shapes=[pltpu.VMEM((tm, D), BF16)],
        compiler_params=_cparams("parallel", "arbitrary"),
        name="rms_inproj",
    )(x, g.reshape(1, D), w)


def _rwkv_prep_kernel(p_ref, prev_ref, mu_ref, w0_ref, a0_ref, kk_ref, ka_ref, rk_ref,
                      w2_ref, a2_ref, g2_ref,
                      a_out, b_out, w_out, k_out, rp_out, v_out, g_out, kr_out, bon_out,
                      last_ref, *, dr):
    @pl.when(pl.program_id(1) == 0)
    def _():
        last_ref[...] = prev_ref[...]

    p = p_ref[...]
    tb = p.shape[0]
    row = lax.broadcasted_iota(jnp.int32, p.shape, 0)
    shifted = jnp.where(row == 0, last_ref[...], pltpu.roll(p, 1, axis=0))
    last_ref[...] = p[tb - 1:tb, :]
    xm = p + (shifted - p) * mu_ref[...]

    lo = xm[:, 3 * dr:]
    wl = jnp.dot(jnp.tanh(lo).astype(BF16), w2_ref[...], preferred_element_type=F32)
    al = jnp.dot(lo.astype(BF16), a2_ref[...], preferred_element_type=F32)
    g_out[...] = jnp.dot(_sigmoid(lo).astype(BF16), g2_ref[...], preferred_element_type=F32)

    ones = _pair_ones()
    for c in range(dr // LANES):
        sl = slice(c * LANES, (c + 1) * LANES)
        r = xm[:, c * LANES:(c + 1) * LANES]
        k = xm[:, dr + c * LANES:dr + (c + 1) * LANES]
        v = xm[:, 2 * dr + c * LANES:2 * dr + (c + 1) * LANES]
        decay = jnp.exp(-jnp.exp(-_softplus(-(w0_ref[:, sl] + wl[:, sl])) - 0.5))
        asig = _sigmoid(a0_ref[:, sl] + al[:, sl])
        kk = k * kk_ref[:, sl]
        kk = kk * lax.rsqrt(jnp.maximum(_seg_sum(kk * kk, ones), 1e-24))
        k2 = k * (1.0 + (asig - 1.0) * ka_ref[:, sl])
        a_s = -kk
        b_s = kk * asig
        rp = decay * r + a_s * _seg_sum(b_s * r, ones)
        rk2 = r * k2
        a_out[:, sl] = a_s
        b_out[:, sl] = b_s
        w_out[:, sl] = decay
        k_out[:, sl] = k2
        rp_out[:, sl] = rp
        v_out[:, sl] = v
        kr_out[:, sl] = _seg_sum(rk2, ones)
        bon_out[:, sl] = _seg_sum(rk2 * rk_ref[:, sl], ones)


def _rwkv_prep(proj, shift_prev, nseq, tseq, cr, dr, mu_p, w0, a0, k_k, k_a, r_k, w2p, a2p, g2p):
    M = proj.shape[0]
    tb = _pick_tile(tseq, (128, 64, 32))
    nt = tseq // tb
    lp = cr - 3 * dr
    row = lambda a: a.reshape(1, -1).astype(F32)
    vec_spec = pl.BlockSpec((1, dr), lambda b, i: (0, 0))
    mat_spec = pl.BlockSpec((lp, dr), lambda b, i: (0, 0))
    out_spec = pl.BlockSpec((tb, dr), lambda b, i: (b * nt + i, 0))
    outs = pl.pallas_call(
        functools.partial(_rwkv_prep_kernel, dr=dr),
        out_shape=[jax.ShapeDtypeStruct((M, dr), F32)] * 9,
        grid=(nseq, nt),
        in_specs=[
            pl.BlockSpec((tb, cr), lambda b, i: (b * nt + i, 0)),
            pl.BlockSpec((None, 1, cr), lambda b, i: (b, 0, 0)),
            pl.BlockSpec((1, cr), lambda b, i: (0, 0)),
            vec_spec, vec_spec, vec_spec, vec_spec, vec_spec,
            mat_spec, mat_spec, mat_spec,
        ],
        out_specs=[out_spec] * 9,
        scratch_shapes=[pltpu.VMEM((1, cr), F32)],
        compiler_params=_cparams("parallel", "arbitrary"),
        name="rwkv_prep",
    )(proj, shift_prev, mu_p, row(w0), row(a0), row(k_k), row(k_a), row(r_k), w2p, a2p, g2p)
    return outs


def _pair_transpose(x):
    t2 = jnp.concatenate([x, x], axis=0).T
    lane = lax.broadcasted_iota(jnp.int32, x.shape, 1)
    return jnp.where(lane < RWKV_HEAD, t2[:RWKV_HEAD], t2[RWKV_HEAD:])


def _wkv_kernel(a_ref, b_ref, w_ref, k_ref, rp_ref, v_ref, g_ref, kr_ref, bon_ref,
                s0_ref, lnw_ref, lnb_ref, y_ref, s_out_ref, s_scr, vt_scr, yt_scr, *, tb, npairs):
    i = pl.program_id(1)

    @pl.when(i == 0)
    def _():
        s_scr[...] = s0_ref[...]

    ones = _pair_ones()
    n = RWKV_HEAD
    for p in range(npairs):
        v = v_ref[:, p * LANES:(p + 1) * LANES]
        if tb < n:
            v = jnp.concatenate([v, jnp.zeros((n - tb, LANES), F32)], axis=0)
        vt_scr[p] = _pair_transpose(v)
        yt_scr[p] = jnp.zeros((n, LANES), F32)

    lane = lax.broadcasted_iota(jnp.int32, (n, LANES), 1) % n

    ngroups = 2 if npairs % 2 == 0 else 1
    gsize = npairs // ngroups

    def steps(t8, carry):
        rows = pl.ds(pl.multiple_of(t8 * SUBLANES, SUBLANES), SUBLANES)
        for j in range(SUBLANES):
            mask = lane == t8 * SUBLANES + j
            row = lambda ref, sl: ref[rows, sl][j:j + 1]
            for g in range(ngroups):
                pairs = range(g * gsize, (g + 1) * gsize)
                parts = []
                for p in pairs:
                    sl = slice(p * LANES, (p + 1) * LANES)
                    s = s_scr[p]
                    parts += [(s * row(a_ref, sl)).astype(BF16),
                              (s * row(rp_ref, sl)).astype(BF16),
                              jnp.where(mask, vt_scr[p], 0.0).astype(BF16)]
                res = jnp.dot(jnp.concatenate(parts, axis=0), ones, preferred_element_type=F32)
                for q, p in enumerate(pairs):
                    sl = slice(p * LANES, (p + 1) * LANES)
                    sa, y, vb = (res[(3 * q + c) * n:(3 * q + c + 1) * n] for c in range(3))
                    s_scr[p] = (s_scr[p] * row(w_ref, sl) + sa * row(b_ref, sl)
                                + vb * row(k_ref, sl))
                    yt_scr[p] = jnp.where(mask, y, yt_scr[p])
        return carry

    lax.fori_loop(0, tb // SUBLANES, steps, 0)

    inv_n = 1.0 / n
    for p in range(npairs):
        sl = slice(p * LANES, (p + 1) * LANES)
        v = v_ref[:, sl]
        y = _pair_transpose(yt_scr[p])[:tb] + v * kr_ref[:, sl]
        mean = _seg_sum(y, ones) * inv_n
        d = y - mean
        var = _seg_sum(d * d, ones) * inv_n
        yn = d * lax.rsqrt(var + GN_EPS) * lnw_ref[:, sl] + lnb_ref[:, sl]
        y_ref[:, sl] = ((yn + bon_ref[:, sl] * v) * g_ref[:, sl]).astype(BF16)

    @pl.when(i == pl.num_programs(1) - 1)
    def _():
        s_out_ref[...] = s_scr[...]


def _wkv(prep, s0_pairs, nseq, tseq, dr, ln_w, ln_b):
    M = prep[0].shape[0]
    tb = min(tseq, RWKV_HEAD)
    nt = tseq // tb
    npairs = dr // LANES
    blk = pl.BlockSpec((tb, dr), lambda b, i: (b * nt + i, 0))
    st = pl.BlockSpec((None, npairs, RWKV_HEAD, LANES), lambda b, i: (b, 0, 0, 0))
    vec = pl.BlockSpec((1, dr), lambda b, i: (0, 0))
    return pl.pallas_call(
        functools.partial(_wkv_kernel, tb=tb, npairs=npairs),
        out_shape=[jax.ShapeDtypeStruct((M, dr), BF16),
                   jax.ShapeDtypeStruct(s0_pairs.shape, F32)],
        grid=(nseq, nt),
        in_specs=[blk] * 9 + [st, vec, vec],
        out_specs=[blk, st],
        scratch_shapes=[pltpu.VMEM((npairs, RWKV_HEAD, LANES), F32)] * 3,
        compiler_params=_cparams("parallel", "arbitrary"),
        name="wkv7",
    )(*prep, s0_pairs, ln_w.reshape(1, dr).astype(F32), ln_b.reshape(1, dr).astype(F32))


def _state_to_pairs(s):
    B, H, n, _ = s.shape
    return s.reshape(B, H // 2, 2, n, n).transpose(0, 1, 3, 2, 4).reshape(B, H // 2, n, 2 * n)


def _pairs_to_state(s):
    B, hp, n, _ = s.shape
    return s.reshape(B, hp, n, 2, n).transpose(0, 1, 3, 2, 4).reshape(B, 2 * hp, n, n)


def _bias_kernel(tab_ref, band_ref, dec_ref, *, nrel, qb, dec_t, dec_w):
    h = pl.program_id(0)
    m = lax.broadcasted_iota(jnp.int32, (SUBLANES, TOEP_W), 1)
    dd = jnp.where(m < TOEP_W // 2, m, m - TOEP_W)
    idx = jnp.clip(dd - ATT_CTX, -REL_CLIP, REL_CLIP) + REL_CLIP

    def pick(r, acc):
        return jnp.where(idx == r, tab_ref[h * nrel + r], acc)

    e = lax.fori_loop(0, nrel, pick, jnp.zeros((SUBLANES, TOEP_W), F32))
    eb = jnp.broadcast_to(e[0:1], (qb, TOEP_W))
    toe = pltpu.roll(eb, 0, axis=1, stride=1, stride_axis=0)
    kb = band_ref.shape[-1]
    cq = lax.broadcasted_iota(jnp.int32, (qb, kb), 0) // CHUNK
    ck = lax.broadcasted_iota(jnp.int32, (qb, kb), 1) // CHUNK
    valid = (ck >= cq) & (ck <= cq + N_PAST_CHUNKS)
    band_ref[...] = jnp.where(valid, toe[:, :kb], NEG_INF)
    dec_ref[...] = toe[:dec_t, :dec_w]


def _bias_tiles(table, qb, dec_t, dec_w):
    nrel, H = table.shape
    tab = table.T.reshape(-1).astype(F32)
    return pl.pallas_call(
        functools.partial(_bias_kernel, nrel=nrel, qb=qb, dec_t=dec_t, dec_w=dec_w),
        out_shape=[jax.ShapeDtypeStruct((H, qb, qb + ATT_CTX), F32),
                   jax.ShapeDtypeStruct((H, dec_t, dec_w), F32)],
        grid=(H,),
        in_specs=[pl.BlockSpec(memory_space=pltpu.SMEM)],
        out_specs=[pl.BlockSpec((None, qb, qb + ATT_CTX), lambda h: (h, 0, 0)),
                   pl.BlockSpec((None, dec_t, dec_w), lambda h: (h, 0, 0))],
        compiler_params=_cparams("arbitrary"),
        name="rel_bias_tiles",
    )(tab)


def _head_rms(x, g):
    return x * lax.rsqrt(jnp.mean(x * x, axis=-1, keepdims=True) + RMS_EPS) * g


def _dot_nt(a, b):
    return lax.dot_general(a, b, (((1,), (1,)), ((), ())), preferred_element_type=F32)


def _band_attn_kernel(q_ref, kc_ref, kp_ref, vc_ref, vp_ref, qg_ref, kg_ref, bias_ref,
                      o_ref, knew_ref, *, qb):
    i = pl.program_id(2)
    scale = ATT_HEAD ** -0.5
    qn = _head_rms(q_ref[...], qg_ref[...]).astype(BF16)
    kc = _head_rms(kc_ref[...], kg_ref[...])
    kp = _head_rms(kp_ref[...], kg_ref[...])
    s_p = _dot_nt(qn, kp.astype(BF16)) * scale + bias_ref[:, :ATT_CTX]
    s_p = jnp.where(i > 0, s_p, NEG_INF)
    s_c = _dot_nt(qn, kc.astype(BF16)) * scale + bias_ref[:, ATT_CTX:]
    m = jnp.maximum(jnp.max(s_p, axis=-1, keepdims=True), jnp.max(s_c, axis=-1, keepdims=True))
    p_p = jnp.exp(s_p - m)
    p_c = jnp.exp(s_c - m)
    l = jnp.sum(p_p, axis=-1, keepdims=True) + jnp.sum(p_c, axis=-1, keepdims=True)
    o = (jnp.dot(p_p.astype(BF16), vp_ref[...].astype(BF16), preferred_element_type=F32)
         + jnp.dot(p_c.astype(BF16), vc_ref[...].astype(BF16), preferred_element_type=F32))
    o_ref[...] = (o / l).astype(BF16)

    @pl.when(i == pl.num_programs(2) - 1)
    def _():
        knew_ref[...] = kc


def _band_attention(proj, col0, nseq, tseq, nheads, q_g, k_g, bias_band):
    M = proj.shape[0]
    qb = ATT_CTX
    assert tseq % qb == 0
    nb = tseq // qb
    c0 = col0 // ATT_HEAD
    H = nheads
    cur = lambda off: pl.BlockSpec((qb, ATT_HEAD), lambda b, h, i: (b * nb + i, c0 + off + h))
    prv = lambda off: pl.BlockSpec(
        (qb, ATT_HEAD), lambda b, h, i: (b * nb + jnp.maximum(i - 1, 0), c0 + off + h))
    gspec = pl.BlockSpec((1, ATT_HEAD), lambda b, h, i: (0, 0))
    return pl.pallas_call(
        functools.partial(_band_attn_kernel, qb=qb),
        out_shape=[jax.ShapeDtypeStruct((M, H * ATT_HEAD), BF16),
                   jax.ShapeDtypeStruct((nseq, qb, H * ATT_HEAD), F32)],
        grid=(nseq, H, nb),
        in_specs=[cur(0), cur(H), prv(H), cur(2 * H), prv(2 * H), gspec, gspec,
                  pl.BlockSpec((None, qb, qb + ATT_CTX), lambda b, h, i: (h, 0, 0))],
        out_specs=[pl.BlockSpec((qb, ATT_HEAD), lambda b, h, i: (b * nb + i, h)),
                   pl.BlockSpec((None, qb, ATT_HEAD), lambda b, h, i: (b, 0, h))],
        compiler_params=_cparams("parallel", "parallel", "arbitrary"),
        name="band_attention",
    )(proj, proj, proj, proj, proj, q_g.reshape(1, -1), k_g.reshape(1, -1), bias_band)


def _cached_attn_kernel(q_ref, kn_ref, vn_ref, kc_ref, vc_ref, qg_ref, kg_ref, bias_ref,
                        o_ref, knew_ref, *, t):
    scale = ATT_HEAD ** -0.5
    L = kc_ref.shape[0]
    qn = _head_rms(q_ref[...], qg_ref[...]).astype(BF16)
    kn = _head_rms(kn_ref[...], kg_ref[...])
    knew_ref[...] = kn
    s_c = _dot_nt(qn, kc_ref[...].astype(BF16)) * scale + bias_ref[:, :L]
    s_n = _dot_nt(qn, kn.astype(BF16)) * scale + bias_ref[:, L:L + t]
    m = jnp.maximum(jnp.max(s_c, axis=-1, keepdims=True), jnp.max(s_n, axis=-1, keepdims=True))
    p_c = jnp.exp(s_c - m)
    p_n = jnp.exp(s_n - m)
    l = jnp.sum(p_c, axis=-1, keepdims=True) + jnp.sum(p_n, axis=-1, keepdims=True)
    o = (jnp.dot(p_c.astype(BF16), vc_ref[...].astype(BF16), preferred_element_type=F32)
         + jnp.dot(p_n.astype(BF16), vn_ref[...].astype(BF16), preferred_element_type=F32))
    o_ref[...] = (o / l).astype(BF16)


def _cached_attention(proj, col0, nseq, tseq, nheads, k_cache, v_cache, q_g, k_g, bias_dec):
    M = proj.shape[0]
    H = nheads
    L = k_cache.shape[1]
    c0 = col0 // ATT_HEAD
    new = lambda off: pl.BlockSpec((tseq, ATT_HEAD), lambda b, h: (b, c0 + off + h))
    cache = pl.BlockSpec((None, L, ATT_HEAD), lambda b, h: (b, 0, h))
    gspec = pl.BlockSpec((1, ATT_HEAD), lambda b, h: (0, 0))
    out = pl.BlockSpec((tseq, ATT_HEAD), lambda b, h: (b, h))
    return pl.pallas_call(
        functools.partial(_cached_attn_kernel, t=tseq),
        out_shape=[jax.ShapeDtypeStruct((M, H * ATT_HEAD), BF16),
                   jax.ShapeDtypeStruct((M, H * ATT_HEAD), F32)],
        grid=(nseq, H),
        in_specs=[new(0), new(H), new(2 * H), cache, cache, gspec, gspec,
                  pl.BlockSpec((None,) + bias_dec.shape[1:], lambda b, h: (h, 0, 0))],
        out_specs=[out, out],
        compiler_params=_cparams("parallel", "arbitrary"),
        name="cached_attention",
    )(proj, proj, proj, k_cache, v_cache, q_g.reshape(1, -1), k_g.reshape(1, -1), bias_dec)


def _outproj_kernel(yr_ref, ya_ref, w_ref, x_ref, h_ref, *, dr):
    acc = jnp.dot(yr_ref[...], w_ref[:dr, :], preferred_element_type=F32)
    acc += jnp.dot(ya_ref[...], w_ref[dr:, :], preferred_element_type=F32)
    h_ref[...] = x_ref[...] + acc


def _outproj(y_r, y_a, w_out, x):
    M, D = x.shape
    dr, da = y_r.shape[1], y_a.shape[1]
    tm = _pick_tile(M, (512, 256, 128, 64, 32))
    tn = _pick_tile(D, (512, 256, 128))
    return pl.pallas_call(
        functools.partial(_outproj_kernel, dr=dr),
        out_shape=jax.ShapeDtypeStruct((M, D), F32),
        grid=(M // tm, D // tn),
        in_specs=[
            pl.BlockSpec((tm, dr), lambda m, n: (m, 0)),
            pl.BlockSpec((tm, da), lambda m, n: (m, 0)),
            pl.BlockSpec((dr + da, tn), lambda m, n: (0, n)),
            pl.BlockSpec((tm, tn), lambda m, n: (m, n)),
        ],
        out_specs=pl.BlockSpec((tm, tn), lambda m, n: (m, n)),
        compiler_params=_cparams("parallel", "arbitrary"),
        name="out_proj",
    )(y_r, y_a, w_out, x)


def _ffn_up_kernel(h_ref, g_ref, wg_ref, wv_ref, cw_ref, cb_ref, b1_ref, b2_ref,
                   act_ref, gate_ref, hn_ref, carry_ref, *, tseq, blocks_per_seq):
    m = pl.program_id(0)
    n = pl.program_id(1)

    @pl.when(n == 0)
    def _():
        x = h_ref[...]
        ms = jnp.mean(x * x, axis=-1, keepdims=True)
        hn_ref[...] = (x * lax.rsqrt(ms + RMS_EPS) * g_ref[...]).astype(BF16)

    hn = hn_ref[...]
    gate = jnp.dot(hn, wg_ref[...], preferred_element_type=F32)
    val = jnp.dot(hn, wv_ref[...], preferred_element_type=F32)
    tm = gate.shape[0]
    row = lax.broadcasted_iota(jnp.int32, gate.shape, 0)
    r1 = pltpu.roll(gate, 1, axis=0)
    r2 = pltpu.roll(gate, 2, axis=0)
    if blocks_per_seq >= 1:
        first = m % blocks_per_seq == 0
        prev = carry_ref[n]
        p2 = jnp.where(first, b2_ref[0:1, :], prev[SUBLANES - 2:SUBLANES - 1, :])
        p1 = jnp.where(first, b1_ref[0:1, :], prev[SUBLANES - 1:SUBLANES, :])
        g1 = jnp.where(row == 0, p1, r1)
        g2 = jnp.where(row == 0, p2, jnp.where(row == 1, p1, r2))
        carry_ref[n] = gate[tm - SUBLANES:, :]
        gate_ref[...] = gate[tm - (FFN_CONV - 1):, :]
    else:
        t = row % tseq
        g1 = jnp.where(t == 0, b1_ref[...], r1)
        g2 = jnp.where(t < 2, b2_ref[...], r2)
        gate_ref[...] = gate
    conv = cb_ref[...] + g2 * cw_ref[0:1, :] + g1 * cw_ref[1:2, :] + gate * cw_ref[2:3, :]
    act_ref[...] = (conv * _sigmoid(conv) * val).astype(BF16)


def _ffn_up(h, g, w_up, conv_w, conv_b, conv_prev, nseq, tseq):
    M, D = h.shape
    Fd = w_up.shape[1] // 2
    tn = _pick_tile(Fd, (256, 128))
    nn = Fd // tn
    if tseq >= 512:
        tm = 512
        bps = tseq // tm
        b1 = conv_prev[:, 1:2, :]
        b2 = conv_prev[:, 0:1, :]
        bspec = pl.BlockSpec((None, 1, tn), lambda m, n: (m // bps, 0, n))
        gate_shape = jax.ShapeDtypeStruct((M // tm, FFN_CONV - 1, Fd), F32)
        gate_spec = pl.BlockSpec((None, FFN_CONV - 1, tn), lambda m, n: (m, 0, n))
    else:
        tm = M
        bps = 0
        z = jnp.zeros((nseq, tseq, Fd), F32)
        b1 = z.at[:, 0].set(conv_prev[:, 1]).reshape(M, Fd)
        b2 = z.at[:, 0].set(conv_prev[:, 0]).at[:, 1].set(conv_prev[:, 1]).reshape(M, Fd)
        bspec = pl.BlockSpec((tm, tn), lambda m, n: (m, n))
        gate_shape = jax.ShapeDtypeStruct((M, Fd), F32)
        gate_spec = pl.BlockSpec((tm, tn), lambda m, n: (m, n))
    act, gate = pl.pallas_call(
        functools.partial(_ffn_up_kernel, tseq=tseq, blocks_per_seq=bps),
        out_shape=[jax.ShapeDtypeStruct((M, Fd), BF16), gate_shape],
        grid=(M // tm, nn),
        in_specs=[
            pl.BlockSpec((tm, D), lambda m, n: (m, 0)),
            pl.BlockSpec((1, D), lambda m, n: (0, 0)),
            pl.BlockSpec((D, tn), lambda m, n: (0, n)),
            pl.BlockSpec((D, tn), lambda m, n: (0, nn + n)),
            pl.BlockSpec((FFN_CONV, tn), lambda m, n: (0, n)),
            pl.BlockSpec((1, tn), lambda m, n: (0, n)),
            bspec, bspec,
        ],
        out_specs=[pl.BlockSpec((tm, tn), lambda m, n: (m, n)), gate_spec],
        scratch_shapes=[pltpu.VMEM((tm, D), BF16), pltpu.VMEM((nn, SUBLANES, tn), F32)],
        compiler_params=_cparams("arbitrary", "arbitrary"),
        name="ffn_up",
    )(h, g.reshape(1, D), w_up, w_up, conv_w, conv_b.reshape(1, Fd), b1, b2)
    if bps == 0:
        gate = gate.reshape(nseq, tseq, Fd)[:, tseq - (FFN_CONV - 1):]
    else:
        gate = gate[bps - 1::bps]
    return act, gate


def _ffn_down_kernel(a_ref, w_ref, h_ref, o_ref):
    o_ref[...] = h_ref[...] + jnp.dot(a_ref[...], w_ref[...], preferred_element_type=F32)


def _ffn_down(act, w_down, h):
    M, Fd = act.shape
    D = w_down.shape[1]
    tm = _pick_tile(M, (512, 256, 128, 64, 32))
    tn = _pick_tile(D, (256, 128))
    return pl.pallas_call(
        _ffn_down_kernel,
        out_shape=jax.ShapeDtypeStruct((M, D), F32),
        grid=(M // tm, D // tn),
        in_specs=[
            pl.BlockSpec((tm, Fd), lambda m, n: (m, 0)),
            pl.BlockSpec((Fd, tn), lambda m, n: (0, n)),
            pl.BlockSpec((tm, tn), lambda m, n: (m, n)),
        ],
        out_specs=pl.BlockSpec((tm, tn), lambda m, n: (m, n)),
        compiler_params=_cparams("parallel", "arbitrary"),
        name="ffn_down",
    )(act, w_down, h)


def _layer_weights(lp):
    (attn_norm_g, w_in, mu, w0, w2, a0, a2, g2, k_k, k_a, r_k, ln_w, ln_b, q_g, k_g, rel_bias,
     w_out, ffn_norm_g, w_up, conv_w, conv_b, w_down) = lp
    D = w_in.shape[0]
    dr = w0.shape[0]
    dl, al, gl = w2.shape[0], a2.shape[0], g2.shape[0]
    nrc = 3 * dr + dl + al + gl
    lpad = _round_up(dl + al + gl, LANES)
    cr = 3 * dr + lpad
    zc = jnp.zeros((D, cr - nrc), w_in.dtype)
    w_in_p = jnp.concatenate([w_in[:, :nrc], zc, w_in[:, nrc:]], axis=1).astype(BF16)
    mu_p = jnp.concatenate([mu, jnp.zeros((cr - nrc,), mu.dtype)]).reshape(1, cr)
    zrow = lambda n: jnp.zeros((n, dr), F32)
    w2p = jnp.concatenate([w2, zrow(lpad - dl)], axis=0).astype(BF16)
    a2p = jnp.concatenate([zrow(dl), a2, zrow(lpad - dl - al)], axis=0).astype(BF16)
    g2p = jnp.concatenate([zrow(dl + al), g2, zrow(lpad - dl - al - gl)], axis=0).astype(BF16)
    return dict(attn_norm_g=attn_norm_g, w_in_p=w_in_p, mu_p=mu_p, w0=w0, a0=a0, k_k=k_k, k_a=k_a,
                r_k=r_k.reshape(-1), w2p=w2p, a2p=a2p, g2p=g2p, ln_w=ln_w, ln_b=ln_b, q_g=q_g, k_g=k_g,
                rel_bias=rel_bias, w_out=w_out.astype(BF16), ffn_norm_g=ffn_norm_g,
                w_up=w_up.astype(BF16), conv_w=conv_w, conv_b=conv_b, w_down=w_down.astype(BF16),
                nrc=nrc, cr=cr, dr=dr)


def _block(x, shift_prev, wkv_prev, conv_prev, k_cache, v_cache, W, bias_band, bias_dec):
    B, T, D = x.shape
    M = B * T
    dr, cr, nrc = W["dr"], W["cr"], W["nrc"]
    H = (D - dr) // ATT_HEAD
    xf = x.reshape(M, D)
    proj = _rms_matmul(xf, W["attn_norm_g"], W["w_in_p"])

    shift_p = jnp.concatenate([shift_prev, jnp.zeros((B, 1, cr - nrc), F32)], axis=-1)
    prep = _rwkv_prep(proj, shift_p, B, T, cr, dr, W["mu_p"], W["w0"], W["a0"], W["k_k"], W["k_a"],
                      W["r_k"], W["w2p"], W["a2p"], W["g2p"])
    y_r, s_pairs = _wkv(prep, _state_to_pairs(wkv_prev), B, T, dr, W["ln_w"], W["ln_b"])
    wkv_new = _pairs_to_state(s_pairs)
    shift_new = proj.reshape(B, T, -1)[:, T - 1:, :nrc]

    da = H * ATT_HEAD
    if k_cache is None:
        y_a, k_new = _band_attention(proj, cr, B, T, H, W["q_g"], W["k_g"], bias_band)
        keep = min(ATT_CTX, T)
        k_new = k_new.reshape(B, keep, H, ATT_HEAD)
        v_new = proj.reshape(B, T, -1)[:, T - keep:, cr + 2 * da:].reshape(B, keep, H, ATT_HEAD)
    else:
        L = k_cache.shape[1]
        y_a, k_new = _cached_attention(proj, cr, B, T, H, k_cache.reshape(B, L, da),
                                       v_cache.reshape(B, L, da), W["q_g"], W["k_g"], bias_dec)
        k_new = k_new.reshape(B, T, H, ATT_HEAD)
        v_new = proj.reshape(B, T, -1)[:, :, cr + 2 * da:].reshape(B, T, H, ATT_HEAD)

    h = _outproj(y_r, y_a, W["w_out"], xf)
    act, conv_new = _ffn_up(h, W["ffn_norm_g"], W["w_up"], W["conv_w"], W["conv_b"], conv_prev, B, T)
    y = _ffn_down(act, W["w_down"], h)
    return y.reshape(B, T, D), shift_new, wkv_new, k_new, v_new, conv_new


def kernel(x_prompt, x_sample, state_rwkv_shift, state_rwkv_wkv, cache_attn_k, cache_attn_v,
           state_ffn_conv, attn_norm_g, w_in, rwkv_mu, rwkv_w0, rwkv_w2, rwkv_a0, rwkv_a2, rwkv_g2,
           rwkv_k_k, rwkv_k_a, rwkv_r_k, rwkv_ln_w, rwkv_ln_b, att_q_norm_g, att_k_norm_g,
           att_rel_bias, w_out, ffn_norm_g, ffn_w_up, ffn_conv_w, ffn_conv_b, ffn_w_down):
    depth = w_in.shape[0]
    B = x_prompt.shape[0]
    xp, xs = x_prompt, x_sample
    dec_t = x_sample.shape[1]
    L = cache_attn_k.shape[2]
    assert L == ATT_CTX
    dec_w = _round_up(L + dec_t, LANES)
    p_states = ([], [], [], [], [])
    s_states = ([], [], [], [], [])
    for l in range(depth):
        lp = (attn_norm_g[l], w_in[l], rwkv_mu[l], rwkv_w0[l], rwkv_w2[l], rwkv_a0[l], rwkv_a2[l],
              rwkv_g2[l], rwkv_k_k[l], rwkv_k_a[l], rwkv_r_k[l], rwkv_ln_w[l], rwkv_ln_b[l],
              att_q_norm_g[l], att_k_norm_g[l], att_rel_bias[l], w_out[l], ffn_norm_g[l],
              ffn_w_up[l], ffn_conv_w[l], ffn_conv_b[l], ffn_w_down[l])
        W = _layer_weights(lp)
        nrc, dr = W["nrc"], W["dr"]
        Fd = ffn_conv_b.shape[-1]
        bias_band, bias_dec = _bias_tiles(att_rel_bias[l], ATT_CTX, dec_t, dec_w)
        shift0 = jnp.zeros((B, 1, nrc), F32)
        wkv0 = jnp.zeros((B, dr // RWKV_HEAD, RWKV_HEAD, RWKV_HEAD), F32)
        conv0 = jnp.zeros((B, FFN_CONV - 1, Fd), F32)
        xp, *ps = _block(xp, shift0, wkv0, conv0, None, None, W, bias_band, bias_dec)
        xs, *ss = _block(xs, state_rwkv_shift[l], state_rwkv_wkv[l], state_ffn_conv[l],
                         cache_attn_k[l], cache_attn_v[l], W, bias_band, bias_dec)
        for i in range(5):
            p_states[i].append(ps[i])
            s_states[i].append(ss[i])
    p_out = [jnp.stack(t, axis=0) for t in p_states]
    s_out = [jnp.stack(t, axis=0) for t in s_states]
    return (xp, xs, *p_out, *s_out)
```

```python
import functools
import math

import jax
import jax.numpy as jnp
from jax import lax
from jax.experimental import pallas as pl
from jax.experimental.pallas import tpu as pltpu

F32 = jnp.float32
BF16 = jnp.bfloat16

RMS_EPS = 1e-6
GN_EPS = 64e-5
NEG_INF = -1e30
CHUNK = 64
N_PAST_CHUNKS = 8
ATT_CTX = N_PAST_CHUNKS * CHUNK
REL_CLIP = 128
RWKV_HEAD = 64
ATT_HEAD = 128
FFN_CONV = 3

LANES = 128
SUBLANES = 8
VMEM_LIMIT = 56 * 1024 * 1024
TOEP_W = 2048


def _cparams(*sem):
    return pltpu.CompilerParams(dimension_semantics=sem, vmem_limit_bytes=VMEM_LIMIT)


def _round_up(n, m):
    return (n + m - 1) // m * m


def _pick_tile(n, candidates):
    for c in candidates:
        if n % c == 0:
            return c
    raise ValueError(f"no tile in {candidates} divides {n}")


def _pair_ones():
    r = lax.broadcasted_iota(jnp.int32, (LANES, LANES), 0) // RWKV_HEAD
    c = lax.broadcasted_iota(jnp.int32, (LANES, LANES), 1) // RWKV_HEAD
    return (r == c).astype(BF16)


def _seg_sum(x, ones):
    hi = x.astype(BF16)
    lo = (x - hi.astype(F32)).astype(BF16)
    return (jnp.dot(hi, ones, preferred_element_type=F32)
            + jnp.dot(lo, ones, preferred_element_type=F32))


def _softplus(z):
    return jnp.maximum(z, 0.0) + jnp.log1p(jnp.exp(-jnp.abs(z)))


def _sigmoid(z):
    return 1.0 / (1.0 + jnp.exp(-z))


def _rms_to_bf16(x_ref, g_ref, xn_ref):
    tm = x_ref.shape[0]
    rc = min(tm, 256)
    for c in range(tm // rc):
        rows = slice(c * rc, (c + 1) * rc)
        x = x_ref[rows, :]
        ms = jnp.mean(x * x, axis=-1, keepdims=True)
        xn_ref[rows, :] = (x * lax.rsqrt(ms + RMS_EPS) * g_ref[...]).astype(BF16)


def _rms_matmul_kernel(x_ref, g_ref, wa_ref, wb_ref, o_ref, xn_ref, *, na):
    n = pl.program_id(1)

    @pl.when(n == 0)
    def _():
        _rms_to_bf16(x_ref, g_ref, xn_ref)

    @pl.when(n < na)
    def _():
        o_ref[...] = jnp.dot(xn_ref[...], wa_ref[...], preferred_element_type=F32)

    @pl.when(n >= na)
    def _():
        o_ref[...] = jnp.dot(xn_ref[...], wb_ref[...], preferred_element_type=F32)


def _rms_matmul(x, g, wa, wb):
    M, D = x.shape
    Na, Nb = wa.shape[1], wb.shape[1]
    tm = _pick_tile(M, (1024, 512, 256, 128, 64, 32))
    tn = _pick_tile(math.gcd(Na, Nb), (512, 256, 128))
    na, nb = Na // tn, Nb // tn
    return pl.pallas_call(
        functools.partial(_rms_matmul_kernel, na=na),
        out_shape=jax.ShapeDtypeStruct((M, Na + Nb), F32),
        grid=(M // tm, na + nb),
        in_specs=[
            pl.BlockSpec((tm, D), lambda m, n: (m, 0), pipeline_mode=pl.Buffered(1)),
            pl.BlockSpec((1, D), lambda m, n: (0, 0)),
            pl.BlockSpec((D, tn), lambda m, n: (0, jnp.minimum(n, na - 1))),
            pl.BlockSpec((D, tn), lambda m, n: (0, jnp.maximum(n - na, 0))),
        ],
        out_specs=pl.BlockSpec((tm, tn), lambda m, n: (m, n)),
        scratch_shapes=[pltpu.VMEM((tm, D), BF16)],
        compiler_params=_cparams("parallel", "arbitrary"),
        name="rms_inproj",
    )(x, g.reshape(1, D), wa, wb)


def _rwkv_prep_kernel(p_ref, prev_ref, mu_ref, w0_ref, a0_ref, kk_ref, ka_ref, rk_ref,
                      w2_ref, a2_ref, g2_ref,
                      a_out, b_out, w_out, k_out, rp_out, v_out, g_out, kr_out, bon_out,
                      last_ref, *, dr):
    @pl.when(pl.program_id(1) == 0)
    def _():
        last_ref[...] = prev_ref[...]

    p = p_ref[...]
    tb = p.shape[0]
    row = lax.broadcasted_iota(jnp.int32, p.shape, 0)
    shifted = jnp.where(row == 0, last_ref[...], pltpu.roll(p, 1, axis=0))
    last_ref[...] = p[tb - 1:tb, :]
    xm = p + (shifted - p) * mu_ref[...]

    lo = xm[:, 3 * dr:]
    wl = jnp.dot(jnp.tanh(lo).astype(BF16), w2_ref[...], preferred_element_type=F32)
    al = jnp.dot(lo.astype(BF16), a2_ref[...], preferred_element_type=F32)
    g_out[...] = jnp.dot(_sigmoid(lo).astype(BF16), g2_ref[...], preferred_element_type=F32)

    ones = _pair_ones()
    for c in range(dr // LANES):
        sl = slice(c * LANES, (c + 1) * LANES)
        r = xm[:, c * LANES:(c + 1) * LANES]
        k = xm[:, dr + c * LANES:dr + (c + 1) * LANES]
        v = xm[:, 2 * dr + c * LANES:2 * dr + (c + 1) * LANES]
        decay = jnp.exp(-jnp.exp(-_softplus(-(w0_ref[:, sl] + wl[:, sl])) - 0.5))
        asig = _sigmoid(a0_ref[:, sl] + al[:, sl])
        kk = k * kk_ref[:, sl]
        kk = kk * lax.rsqrt(jnp.maximum(_seg_sum(kk * kk, ones), 1e-24))
        k2 = k * (1.0 + (asig - 1.0) * ka_ref[:, sl])
        a_s = -kk
        b_s = kk * asig
        rp = decay * r + a_s * _seg_sum(b_s * r, ones)
        rk2 = r * k2
        a_out[:, sl] = a_s
        b_out[:, sl] = b_s
        w_out[:, sl] = decay
        k_out[:, sl] = k2
        rp_out[:, sl] = rp
        v_out[:, sl] = v
        kr_out[:, sl] = _seg_sum(rk2, ones)
        bon_out[:, sl] = _seg_sum(rk2 * rk_ref[:, sl], ones)


def _rwkv_prep(proj, shift_prev, nseq, tseq, cr, dr, mu_p, w0, a0, k_k, k_a, r_k, w2p, a2p, g2p):
    M = proj.shape[0]
    tb = _pick_tile(tseq, (128, 64, 32))
    nt = tseq // tb
    lp = cr - 3 * dr
    row = lambda a: a.reshape(1, -1).astype(F32)
    vec_spec = pl.BlockSpec((1, dr), lambda b, i: (0, 0))
    mat_spec = pl.BlockSpec((lp, dr), lambda b, i: (0, 0))
    out_spec = pl.BlockSpec((tb, dr), lambda b, i: (b * nt + i, 0))
    outs = pl.pallas_call(
        functools.partial(_rwkv_prep_kernel, dr=dr),
        out_shape=[jax.ShapeDtypeStruct((M, dr), F32)] * 9,
        grid=(nseq, nt),
        in_specs=[
            pl.BlockSpec((tb, cr), lambda b, i: (b * nt + i, 0)),
            pl.BlockSpec((None, 1, cr), lambda b, i: (b, 0, 0)),
            pl.BlockSpec((1, cr), lambda b, i: (0, 0)),
            vec_spec, vec_spec, vec_spec, vec_spec, vec_spec,
            mat_spec, mat_spec, mat_spec,
        ],
        out_specs=[out_spec] * 9,
        scratch_shapes=[pltpu.VMEM((1, cr), F32)],
        compiler_params=_cparams("parallel", "arbitrary"),
        name="rwkv_prep",
    )(proj, shift_prev, mu_p, row(w0), row(a0), row(k_k), row(k_a), row(r_k), w2p, a2p, g2p)
    return outs


def _pair_transpose(x):
    t2 = jnp.concatenate([x, x], axis=0).T
    lane = lax.broadcasted_iota(jnp.int32, x.shape, 1)
    return jnp.where(lane < RWKV_HEAD, t2[:RWKV_HEAD], t2[RWKV_HEAD:])


def _wkv_kernel(a_ref, b_ref, w_ref, k_ref, rp_ref, v_ref, g_ref, kr_ref, bon_ref,
                s0_ref, lnw_ref, lnb_ref, y_ref, s_out_ref, s_scr, vt_scr, yt_scr, *, tb, npairs):
    i = pl.program_id(1)
    n = RWKV_HEAD

    @pl.when(i == 0)
    def _():
        for p in range(npairs):
            s_scr[p] = jnp.concatenate([s0_ref[2 * p], s0_ref[2 * p + 1]], axis=1)

    ones = _pair_ones()
    for p in range(npairs):
        v = v_ref[:, p * LANES:(p + 1) * LANES]
        if tb < n:
            v = jnp.concatenate([v, jnp.zeros((n - tb, LANES), F32)], axis=0)
        vt_scr[p] = _pair_transpose(v).astype(BF16)
        yt_scr[p] = jnp.zeros((n, LANES), F32)

    lane = lax.broadcasted_iota(jnp.int32, (n, LANES), 1) % n

    ngroups = 2 if npairs % 2 == 0 else 1
    gsize = npairs // ngroups
    sel_t = lax.broadcasted_iota(jnp.int32, (LANES, 2 * LANES), 0) % n
    sel_step = lax.broadcasted_iota(jnp.int32, (LANES, 2 * LANES), 1) // LANES
    sel_same_head = (lax.broadcasted_iota(jnp.int32, (LANES, 2 * LANES), 0) // n
                     == lax.broadcasted_iota(jnp.int32, (LANES, 2 * LANES), 1) % LANES // n)

    def steps(t8, carry):
        rows = pl.ds(pl.multiple_of(t8 * SUBLANES, SUBLANES), SUBLANES)
        for j in range(SUBLANES):
            t = t8 * SUBLANES + j
            mask = lane == t
            row = lambda ref, p: ref[rows, p * LANES:(p + 1) * LANES][j:j + 1]
            if j % 2 == 0:
                sel = (sel_same_head & (sel_t == t + sel_step)).astype(BF16)
                vb2 = jnp.dot(vt_scr[...].reshape(npairs * n, LANES), sel,
                              preferred_element_type=F32)
            for g in range(ngroups):
                pairs = range(g * gsize, (g + 1) * gsize)
                parts = []
                for p in pairs:
                    s = s_scr[p]
                    parts += [(s * row(a_ref, p)).astype(BF16), (s * row(rp_ref, p)).astype(BF16)]
                res = jnp.dot(jnp.concatenate(parts, axis=0), ones, preferred_element_type=F32)
                for q, p in enumerate(pairs):
                    sa, y = res[2 * q * n:(2 * q + 1) * n], res[(2 * q + 1) * n:(2 * q + 2) * n]
                    vb = vb2[p * n:(p + 1) * n, (j % 2) * LANES:(j % 2 + 1) * LANES]
                    s_scr[p] = (s_scr[p] * row(w_ref, p) + sa * row(b_ref, p)
                                + vb * row(k_ref, p))
                    yt_scr[p] = jnp.where(mask, y, yt_scr[p])
        return carry

    lax.fori_loop(0, tb // SUBLANES, steps, 0)

    inv_n = 1.0 / n
    for p in range(npairs):
        sl = slice(p * LANES, (p + 1) * LANES)
        v = v_ref[:, sl]
        y = _pair_transpose(yt_scr[p])[:tb] + v * kr_ref[:, sl]
        mean = _seg_sum(y, ones) * inv_n
        d = y - mean
        var = _seg_sum(d * d, ones) * inv_n
        yn = d * lax.rsqrt(var + GN_EPS) * lnw_ref[:, sl] + lnb_ref[:, sl]
        y_ref[:, sl] = ((yn + bon_ref[:, sl] * v) * g_ref[:, sl]).astype(BF16)

    @pl.when(i == pl.num_programs(1) - 1)
    def _():
        for p in range(npairs):
            s = s_scr[p]
            s_out_ref[2 * p] = s[:, :n]
            s_out_ref[2 * p + 1] = s[:, n:]


def _wkv(prep, s0, nseq, tseq, dr, ln_w, ln_b):
    M = prep[0].shape[0]
    tb = min(tseq, RWKV_HEAD)
    nt = tseq // tb
    npairs = dr // LANES
    assert npairs % 2 == 0
    blk = pl.BlockSpec((tb, dr), lambda b, i: (b * nt + i, 0))
    st = pl.BlockSpec((None, 2 * npairs, RWKV_HEAD, RWKV_HEAD), lambda b, i: (b, 0, 0, 0))
    vec = pl.BlockSpec((1, dr), lambda b, i: (0, 0))
    return pl.pallas_call(
        functools.partial(_wkv_kernel, tb=tb, npairs=npairs),
        out_shape=[jax.ShapeDtypeStruct((M, dr), BF16),
                   jax.ShapeDtypeStruct(s0.shape, F32)],
        grid=(nseq, nt),
        in_specs=[blk] * 9 + [st, vec, vec],
        out_specs=[blk, st],
        scratch_shapes=[pltpu.VMEM((npairs, RWKV_HEAD, LANES), F32),
                        pltpu.VMEM((npairs, RWKV_HEAD, LANES), BF16),
                        pltpu.VMEM((npairs, RWKV_HEAD, LANES), F32)],
        compiler_params=_cparams("parallel", "arbitrary"),
        name="wkv7",
    )(*prep, s0, ln_w.reshape(1, dr).astype(F32), ln_b.reshape(1, dr).astype(F32))


def _bias_kernel(tab_ref, band_ref, dec_ref, *, nrel, qb, dec_t, dec_w):
    h = pl.program_id(0)
    m = lax.broadcasted_iota(jnp.int32, (SUBLANES, TOEP_W), 1)
    dd = jnp.where(m < TOEP_W // 2, m, m - TOEP_W)
    idx = jnp.clip(dd - ATT_CTX, -REL_CLIP, REL_CLIP) + REL_CLIP

    def pick(r, acc):
        return jnp.where(idx == r, tab_ref[h * nrel + r], acc)

    e = lax.fori_loop(0, nrel, pick, jnp.zeros((SUBLANES, TOEP_W), F32))
    eb = jnp.broadcast_to(e[0:1], (qb, TOEP_W))
    toe = pltpu.roll(eb, 0, axis=1, stride=1, stride_axis=0)
    kb = band_ref.shape[-1]
    cq = lax.broadcasted_iota(jnp.int32, (qb, kb), 0) // CHUNK
    ck = lax.broadcasted_iota(jnp.int32, (qb, kb), 1) // CHUNK
    valid = (ck >= cq) & (ck <= cq + N_PAST_CHUNKS)
    band_ref[...] = jnp.where(valid, toe[:, :kb], NEG_INF)
    dec_ref[...] = toe[:dec_t, :dec_w]


def _bias_tiles(table, qb, dec_t, dec_w):
    nrel, H = table.shape
    tab = table.T.reshape(-1).astype(F32)
    return pl.pallas_call(
        functools.partial(_bias_kernel, nrel=nrel, qb=qb, dec_t=dec_t, dec_w=dec_w),
        out_shape=[jax.ShapeDtypeStruct((H, qb, qb + ATT_CTX), F32),
                   jax.ShapeDtypeStruct((H, dec_t, dec_w), F32)],
        grid=(H,),
        in_specs=[pl.BlockSpec(memory_space=pltpu.SMEM)],
        out_specs=[pl.BlockSpec((None, qb, qb + ATT_CTX), lambda h: (h, 0, 0)),
                   pl.BlockSpec((None, dec_t, dec_w), lambda h: (h, 0, 0))],
        compiler_params=_cparams("arbitrary"),
        name="rel_bias_tiles",
    )(tab)


def _head_rms(x, g):
    return x * lax.rsqrt(jnp.mean(x * x, axis=-1, keepdims=True) + RMS_EPS) * g


def _dot_nt(a, b):
    return lax.dot_general(a, b, (((1,), (1,)), ((), ())), preferred_element_type=F32)


def _heads_per_step(nheads, col0, most):
    for hps in (4, 2, 1):
        if hps <= most and nheads % hps == 0 and col0 % (hps * ATT_HEAD) == 0:
            return hps
    raise ValueError("attention columns are not head aligned")


def _softmax_pv(s_a, s_b, v_a, v_b):
    m = jnp.maximum(jnp.max(s_a, axis=-1, keepdims=True), jnp.max(s_b, axis=-1, keepdims=True))
    p_a = jnp.exp(s_a - m)
    p_b = jnp.exp(s_b - m)
    l = jnp.sum(p_a, axis=-1, keepdims=True) + jnp.sum(p_b, axis=-1, keepdims=True)
    o = (jnp.dot(p_a.astype(BF16), v_a.astype(BF16), preferred_element_type=F32)
         + jnp.dot(p_b.astype(BF16), v_b.astype(BF16), preferred_element_type=F32))
    return o / l


def _band_attn_kernel(q_ref, kc_ref, kp_ref, vc_ref, vp_ref, qg_ref, kg_ref, bias_ref,
                      o_ref, knew_ref, *, hps):
    i = pl.program_id(2)
    scale = ATT_HEAD ** -0.5
    for hh in range(hps):
        sl = slice(hh * ATT_HEAD, (hh + 1) * ATT_HEAD)
        qn = _head_rms(q_ref[:, sl], qg_ref[...]).astype(BF16)
        kc = _head_rms(kc_ref[:, sl], kg_ref[...])
        kp = _head_rms(kp_ref[:, sl], kg_ref[...])
        s_p = _dot_nt(qn, kp.astype(BF16)) * scale + bias_ref[hh, :, :ATT_CTX]
        s_p = jnp.where(i > 0, s_p, NEG_INF)
        s_c = _dot_nt(qn, kc.astype(BF16)) * scale + bias_ref[hh, :, ATT_CTX:]
        o_ref[:, sl] = _softmax_pv(s_p, s_c, vp_ref[:, sl], vc_ref[:, sl]).astype(BF16)

        @pl.when(i == pl.num_programs(2) - 1)
        def _():
            knew_ref[:, sl] = kc


def _band_attention(proj, col0, nseq, tseq, nheads, q_g, k_g, bias_band):
    M = proj.shape[0]
    qb = ATT_CTX
    assert tseq % qb == 0
    nb = tseq // qb
    H = nheads
    hps = _heads_per_step(H, col0, 2)
    w = hps * ATT_HEAD
    c0 = col0 // w
    hb = H // hps
    cur = lambda off: pl.BlockSpec((qb, w), lambda b, h, i: (b * nb + i, c0 + off + h))
    prv = lambda off: pl.BlockSpec(
        (qb, w), lambda b, h, i: (b * nb + jnp.maximum(i - 1, 0), c0 + off + h))
    gspec = pl.BlockSpec((1, ATT_HEAD), lambda b, h, i: (0, 0))
    return pl.pallas_call(
        functools.partial(_band_attn_kernel, hps=hps),
        out_shape=[jax.ShapeDtypeStruct((M, H * ATT_HEAD), BF16),
                   jax.ShapeDtypeStruct((nseq, qb, H * ATT_HEAD), F32)],
        grid=(nseq, hb, nb),
        in_specs=[cur(0), cur(hb), prv(hb), cur(2 * hb), prv(2 * hb), gspec, gspec,
                  pl.BlockSpec((hps, qb, qb + ATT_CTX), lambda b, h, i: (h, 0, 0))],
        out_specs=[pl.BlockSpec((qb, w), lambda b, h, i: (b * nb + i, h)),
                   pl.BlockSpec((None, qb, w), lambda b, h, i: (b, 0, h))],
        compiler_params=_cparams("parallel", "parallel", "arbitrary"),
        name="band_attention",
    )(proj, proj, proj, proj, proj, q_g.reshape(1, -1), k_g.reshape(1, -1), bias_band)


def _cached_attn_kernel(q_ref, kn_ref, vn_ref, kc_ref, vc_ref, qg_ref, kg_ref, bias_ref,
                        o_ref, knew_ref, *, t, hps):
    scale = ATT_HEAD ** -0.5
    L = kc_ref.shape[0]
    for h in range(hps):
        sl = slice(h * ATT_HEAD, (h + 1) * ATT_HEAD)
        qn = _head_rms(q_ref[:, sl], qg_ref[...]).astype(BF16)
        kn = _head_rms(kn_ref[:, sl], kg_ref[...])
        knew_ref[:, sl] = kn
        s_c = _dot_nt(qn, kc_ref[:, sl].astype(BF16)) * scale + bias_ref[h, :, :L]
        s_n = _dot_nt(qn, kn.astype(BF16)) * scale + bias_ref[h, :, L:L + t]
        o_ref[:, sl] = _softmax_pv(s_c, s_n, vc_ref[:, sl], vn_ref[:, sl]).astype(BF16)


def _cached_attention(proj, col0, nseq, tseq, nheads, k_cache, v_cache, q_g, k_g, bias_dec):
    M = proj.shape[0]
    H = nheads
    L = k_cache.shape[1]
    da = H * ATT_HEAD
    hps = _heads_per_step(H, col0, 4)
    w = hps * ATT_HEAD
    c0 = col0 // w
    hb = H // hps
    new = lambda off: pl.BlockSpec((tseq, w), lambda b, h: (b, c0 + off + h))
    cache = pl.BlockSpec((None, L, w), lambda b, h: (b, 0, h))
    gspec = pl.BlockSpec((1, ATT_HEAD), lambda b, h: (0, 0))
    out = pl.BlockSpec((tseq, w), lambda b, h: (b, h))
    return pl.pallas_call(
        functools.partial(_cached_attn_kernel, t=tseq, hps=hps),
        out_shape=[jax.ShapeDtypeStruct((M, da), BF16),
                   jax.ShapeDtypeStruct((M, da), F32)],
        grid=(nseq, hb),
        in_specs=[new(0), new(hb), new(2 * hb), cache, cache, gspec, gspec,
                  pl.BlockSpec((hps,) + bias_dec.shape[1:], lambda b, h: (h, 0, 0))],
        out_specs=[out, out],
        compiler_params=_cparams("parallel", "arbitrary"),
        name="cached_attention",
    )(proj, proj, proj, k_cache, v_cache, q_g.reshape(1, -1), k_g.reshape(1, -1), bias_dec)


def _outproj_kernel(yr_ref, ya_ref, w_ref, x_ref, h_ref, *, dr):
    acc = jnp.dot(yr_ref[...], w_ref[:dr, :], preferred_element_type=F32)
    acc += jnp.dot(ya_ref[...], w_ref[dr:, :], preferred_element_type=F32)
    h_ref[...] = x_ref[...] + acc


def _outproj(y_r, y_a, w_out, x):
    M, D = x.shape
    dr, da = y_r.shape[1], y_a.shape[1]
    tm = _pick_tile(M, (1024, 512, 256, 128, 64, 32))
    tn = _pick_tile(D, (512, 256, 128))
    return pl.pallas_call(
        functools.partial(_outproj_kernel, dr=dr),
        out_shape=jax.ShapeDtypeStruct((M, D), F32),
        grid=(M // tm, D // tn),
        in_specs=[
            pl.BlockSpec((tm, dr), lambda m, n: (m, 0)),
            pl.BlockSpec((tm, da), lambda m, n: (m, 0)),
            pl.BlockSpec((dr + da, tn), lambda m, n: (0, n)),
            pl.BlockSpec((tm, tn), lambda m, n: (m, n)),
        ],
        out_specs=pl.BlockSpec((tm, tn), lambda m, n: (m, n)),
        compiler_params=_cparams("parallel", "arbitrary"),
        name="out_proj",
    )(y_r, y_a, w_out, x)


def _ffn_up_kernel(h_ref, g_ref, wg_ref, wv_ref, cw_ref, cb_ref, b1_ref, b2_ref,
                   act_ref, gate_ref, hn_ref, carry_ref, *, tseq, blocks_per_seq, nsplit):
    m = pl.program_id(0)
    n = pl.program_id(1)

    @pl.when(n == 0)
    def _():
        _rms_to_bf16(h_ref, g_ref, hn_ref)

    tm = hn_ref.shape[0]
    hs = tm // nsplit
    tn = wg_ref.shape[1]
    row8 = lax.broadcasted_iota(jnp.int32, (SUBLANES, tn), 0)
    gate = None
    for c in range(nsplit):
        rows = slice(c * hs, (c + 1) * hs)
        hn = hn_ref[rows, :]
        prev_gate = gate
        gate = jnp.dot(hn, wg_ref[...], preferred_element_type=F32)
        val = jnp.dot(hn, wv_ref[...], preferred_element_type=F32)
        r1 = pltpu.roll(gate, 1, axis=0)
        r2 = pltpu.roll(gate, 2, axis=0)
        if blocks_per_seq >= 1:
            if c == 0:
                first = m % blocks_per_seq == 0
                prev = carry_ref[n]
                p2 = jnp.where(first, b2_ref[0:1, :], prev[SUBLANES - 2:SUBLANES - 1, :])
                p1 = jnp.where(first, b1_ref[0:1, :], prev[SUBLANES - 1:SUBLANES, :])
            else:
                p2 = prev_gate[hs - 2:hs - 1, :]
                p1 = prev_gate[hs - 1:hs, :]
            top1 = jnp.where(row8 == 0, p1, r1[:SUBLANES])
            top2 = jnp.where(row8 == 0, p2, jnp.where(row8 == 1, p1, r2[:SUBLANES]))
            g1 = jnp.concatenate([top1, r1[SUBLANES:]], axis=0)
            g2 = jnp.concatenate([top2, r2[SUBLANES:]], axis=0)
        else:
            t = lax.broadcasted_iota(jnp.int32, gate.shape, 0) % tseq
            g1 = jnp.where(t == 0, b1_ref[rows, :], r1)
            g2 = jnp.where(t < 2, b2_ref[rows, :], r2)
            gate_ref[rows, :] = gate
        conv = cb_ref[...] + g2 * cw_ref[0:1, :] + g1 * cw_ref[1:2, :] + gate * cw_ref[2:3, :]
        act_ref[rows, :] = (conv * _sigmoid(conv) * val).astype(BF16)
    if blocks_per_seq >= 1:
        carry_ref[n] = gate[hs - SUBLANES:, :]
        gate_ref[...] = gate[hs - (FFN_CONV - 1):, :]


def _ffn_up(h, g, w_up, conv_w, conv_b, conv_prev, nseq, tseq):
    M, D = h.shape
    Fd = w_up.shape[1] // 2
    tn = _pick_tile(Fd, (256, 128))
    nn = Fd // tn
    nsplit = 2
    if tseq >= 512:
        tm = _pick_tile(tseq, (1024, 512))
        bps = tseq // tm
        b1 = conv_prev[:, 1:2, :]
        b2 = conv_prev[:, 0:1, :]
        bspec = pl.BlockSpec((None, 1, tn), lambda m, n: (m // bps, 0, n))
        gate_shape = jax.ShapeDtypeStruct((M // tm, FFN_CONV - 1, Fd), F32)
        gate_spec = pl.BlockSpec((None, FFN_CONV - 1, tn), lambda m, n: (m, 0, n))
    else:
        tm = M
        bps = 0
        z = jnp.zeros((nseq, tseq, Fd), F32)
        b1 = z.at[:, 0].set(conv_prev[:, 1]).reshape(M, Fd)
        b2 = z.at[:, 0].set(conv_prev[:, 0]).at[:, 1].set(conv_prev[:, 1]).reshape(M, Fd)
        bspec = pl.BlockSpec((tm, tn), lambda m, n: (m, n))
        gate_shape = jax.ShapeDtypeStruct((M, Fd), F32)
        gate_spec = pl.BlockSpec((tm, tn), lambda m, n: (m, n))
    act, gate = pl.pallas_call(
        functools.partial(_ffn_up_kernel, tseq=tseq, blocks_per_seq=bps, nsplit=nsplit),
        out_shape=[jax.ShapeDtypeStruct((M, Fd), BF16), gate_shape],
        grid=(M // tm, nn),
        in_specs=[
            pl.BlockSpec((tm, D), lambda m, n: (m, 0), pipeline_mode=pl.Buffered(1)),
            pl.BlockSpec((1, D), lambda m, n: (0, 0)),
            pl.BlockSpec((D, tn), lambda m, n: (0, n)),
            pl.BlockSpec((D, tn), lambda m, n: (0, nn + n)),
            pl.BlockSpec((FFN_CONV, tn), lambda m, n: (0, n)),
            pl.BlockSpec((1, tn), lambda m, n: (0, n)),
            bspec, bspec,
        ],
        out_specs=[pl.BlockSpec((tm, tn), lambda m, n: (m, n)), gate_spec],
        scratch_shapes=[pltpu.VMEM((tm, D), BF16), pltpu.VMEM((nn, SUBLANES, tn), F32)],
        compiler_params=_cparams("arbitrary", "arbitrary"),
        name="ffn_up",
    )(h, g.reshape(1, D), w_up, w_up, conv_w, conv_b.reshape(1, Fd), b1, b2)
    if bps == 0:
        gate = gate.reshape(nseq, tseq, Fd)[:, tseq - (FFN_CONV - 1):]
    else:
        gate = gate[bps - 1::bps]
    return act, gate


def _ffn_down_kernel(a_ref, w_ref, h_ref, o_ref):
    o_ref[...] = h_ref[...] + jnp.dot(a_ref[...], w_ref[...], preferred_element_type=F32)


def _ffn_down(act, w_down, h):
    M, Fd = act.shape
    D = w_down.shape[1]
    tm = _pick_tile(M, (512, 256, 128, 64, 32))
    tn = _pick_tile(D, (256, 128))
    return pl.pallas_call(
        _ffn_down_kernel,
        out_shape=jax.ShapeDtypeStruct((M, D), F32),
        grid=(M // tm, D // tn),
        in_specs=[
            pl.BlockSpec((tm, Fd), lambda m, n: (m, 0)),
            pl.BlockSpec((Fd, tn), lambda m, n: (0, n)),
            pl.BlockSpec((tm, tn), lambda m, n: (m, n)),
        ],
        out_specs=pl.BlockSpec((tm, tn), lambda m, n: (m, n)),
        compiler_params=_cparams("parallel", "arbitrary"),
        name="ffn_down",
    )(act, w_down, h)


def _layer_weights(lp):
    (attn_norm_g, w_in, mu, w0, w2, a0, a2, g2, k_k, k_a, r_k, ln_w, ln_b, q_g, k_g, rel_bias,
     w_out, ffn_norm_g, w_up, conv_w, conv_b, w_down) = lp
    D = w_in.shape[0]
    dr = w0.shape[0]
    dl, al, gl = w2.shape[0], a2.shape[0], g2.shape[0]
    nrc = 3 * dr + dl + al + gl
    lpad = _round_up(dl + al + gl, LANES)
    cr = 3 * dr + lpad
    assert cr <= w_in.shape[1]
    w_in_r = w_in[:, :cr].astype(BF16)
    w_in_a = w_in[:, nrc:].astype(BF16)
    mu_p = jnp.concatenate([mu, jnp.zeros((cr - nrc,), mu.dtype)]).reshape(1, cr)
    zrow = lambda n: jnp.zeros((n, dr), F32)
    w2p = jnp.concatenate([w2, zrow(lpad - dl)], axis=0).astype(BF16)
    a2p = jnp.concatenate([zrow(dl), a2, zrow(lpad - dl - al)], axis=0).astype(BF16)
    g2p = jnp.concatenate([zrow(dl + al), g2, zrow(lpad - dl - al - gl)], axis=0).astype(BF16)
    return dict(attn_norm_g=attn_norm_g, w_in_r=w_in_r, w_in_a=w_in_a, mu_p=mu_p, w0=w0, a0=a0, k_k=k_k, k_a=k_a,
                r_k=r_k.reshape(-1), w2p=w2p, a2p=a2p, g2p=g2p, ln_w=ln_w, ln_b=ln_b, q_g=q_g, k_g=k_g,
                rel_bias=rel_bias, w_out=w_out.astype(BF16), ffn_norm_g=ffn_norm_g,
                w_up=w_up.astype(BF16), conv_w=conv_w, conv_b=conv_b, w_down=w_down.astype(BF16),
                nrc=nrc, cr=cr, dr=dr)


def _block(x, shift_prev, wkv_prev, conv_prev, k_cache, v_cache, W, bias_band, bias_dec):
    B, T, D = x.shape
    M = B * T
    dr, cr, nrc = W["dr"], W["cr"], W["nrc"]
    H = (D - dr) // ATT_HEAD
    xf = x.reshape(M, D)
    proj = _rms_matmul(xf, W["attn_norm_g"], W["w_in_r"], W["w_in_a"])

    shift_p = jnp.concatenate([shift_prev, jnp.zeros((B, 1, cr - nrc), F32)], axis=-1)
    prep = _rwkv_prep(proj, shift_p, B, T, cr, dr, W["mu_p"], W["w0"], W["a0"], W["k_k"], W["k_a"],
                      W["r_k"], W["w2p"], W["a2p"], W["g2p"])
    y_r, wkv_new = _wkv(prep, wkv_prev, B, T, dr, W["ln_w"], W["ln_b"])
    shift_new = proj.reshape(B, T, -1)[:, T - 1:, :nrc]

    da = H * ATT_HEAD
    if k_cache is None:
        y_a, k_new = _band_attention(proj, cr, B, T, H, W["q_g"], W["k_g"], bias_band)
        keep = min(ATT_CTX, T)
        k_new = k_new.reshape(B, keep, H, ATT_HEAD)
        v_new = proj.reshape(B, T, -1)[:, T - keep:, cr + 2 * da:].reshape(B, keep, H, ATT_HEAD)
    else:
        L = k_cache.shape[1]
        y_a, k_new = _cached_attention(proj, cr, B, T, H, k_cache.reshape(B, L, da),
                                       v_cache.reshape(B, L, da), W["q_g"], W["k_g"], bias_dec)
        k_new = k_new.reshape(B, T, H, ATT_HEAD)
        v_new = proj.reshape(B, T, -1)[:, :, cr + 2 * da:].reshape(B, T, H, ATT_HEAD)

    h = _outproj(y_r, y_a, W["w_out"], xf)
    act, conv_new = _ffn_up(h, W["ffn_norm_g"], W["w_up"], W["conv_w"], W["conv_b"], conv_prev, B, T)
    y = _ffn_down(act, W["w_down"], h)
    return y.reshape(B, T, D), shift_new, wkv_new, k_new, v_new, conv_new


def kernel(x_prompt, x_sample, state_rwkv_shift, state_rwkv_wkv, cache_attn_k, cache_attn_v,
           state_ffn_conv, attn_norm_g, w_in, rwkv_mu, rwkv_w0, rwkv_w2, rwkv_a0, rwkv_a2, rwkv_g2,
           rwkv_k_k, rwkv_k_a, rwkv_r_k, rwkv_ln_w, rwkv_ln_b, att_q_norm_g, att_k_norm_g,
           att_rel_bias, w_out, ffn_norm_g, ffn_w_up, ffn_conv_w, ffn_conv_b, ffn_w_down):
    depth = w_in.shape[0]
    B = x_prompt.shape[0]
    xp, xs = x_prompt, x_sample
    dec_t = x_sample.shape[1]
    L = cache_attn_k.shape[2]
    assert L == ATT_CTX
    dec_w = _round_up(L + dec_t, LANES)
    p_states = ([], [], [], [], [])
    s_states = ([], [], [], [], [])
    for l in range(depth):
        lp = (attn_norm_g[l], w_in[l], rwkv_mu[l], rwkv_w0[l], rwkv_w2[l], rwkv_a0[l], rwkv_a2[l],
              rwkv_g2[l], rwkv_k_k[l], rwkv_k_a[l], rwkv_r_k[l], rwkv_ln_w[l], rwkv_ln_b[l],
              att_q_norm_g[l], att_k_norm_g[l], att_rel_bias[l], w_out[l], ffn_norm_g[l],
              ffn_w_up[l], ffn_conv_w[l], ffn_conv_b[l], ffn_w_down[l])
        W = _layer_weights(lp)
        nrc, dr = W["nrc"], W["dr"]
        Fd = ffn_conv_b.shape[-1]
        bias_band, bias_dec = _bias_tiles(att_rel_bias[l], ATT_CTX, dec_t, dec_w)
        shift0 = jnp.zeros((B, 1, nrc), F32)
        wkv0 = jnp.zeros((B, dr // RWKV_HEAD, RWKV_HEAD, RWKV_HEAD), F32)
        conv0 = jnp.zeros((B, FFN_CONV - 1, Fd), F32)
        xp, *ps = _block(xp, shift0, wkv0, conv0, None, None, W, bias_band, bias_dec)
        xs, *ss = _block(xs, state_rwkv_shift[l], state_rwkv_wkv[l], state_ffn_conv[l],
                         cache_attn_k[l], cache_attn_v[l], W, bias_band, bias_dec)
        for i in range(5):
            p_states[i].append(ps[i])
            s_states[i].append(ss[i])
    p_out = [jnp.stack(t, axis=0) for t in p_states]
    s_out = [jnp.stack(t, axis=0) for t in s_states]
    return (xp, xs, *p_out, *s_out)
```

```python
import functools
import math

import jax
import jax.numpy as jnp
from jax import lax
from jax.experimental import pallas as pl
from jax.experimental.pallas import tpu as pltpu

F32 = jnp.float32
BF16 = jnp.bfloat16

RMS_EPS = 1e-6
GN_EPS = 64e-5
NEG_INF = -1e30
CHUNK = 64
N_PAST_CHUNKS = 8
ATT_CTX = N_PAST_CHUNKS * CHUNK
REL_CLIP = 128
RWKV_HEAD = 64
ATT_HEAD = 128
FFN_CONV = 3

LANES = 128
SUBLANES = 8
VMEM_LIMIT = 56 * 1024 * 1024
TOEP_W = 2048


def _cparams(*sem):
    return pltpu.CompilerParams(dimension_semantics=sem, vmem_limit_bytes=VMEM_LIMIT)


def _round_up(n, m):
    return (n + m - 1) // m * m


def _pick_tile(n, candidates):
    for c in candidates:
        if n % c == 0:
            return c
    raise ValueError(f"no tile in {candidates} divides {n}")


def _pair_ones():
    r = lax.broadcasted_iota(jnp.int32, (LANES, LANES), 0) // RWKV_HEAD
    c = lax.broadcasted_iota(jnp.int32, (LANES, LANES), 1) // RWKV_HEAD
    return (r == c).astype(BF16)


def _seg_sum(x, ones):
    hi = x.astype(BF16)
    lo = (x - hi.astype(F32)).astype(BF16)
    return (jnp.dot(hi, ones, preferred_element_type=F32)
            + jnp.dot(lo, ones, preferred_element_type=F32))


def _softplus(z):
    return jnp.maximum(z, 0.0) + jnp.log1p(jnp.exp(-jnp.abs(z)))


def _sigmoid(z):
    return 1.0 / (1.0 + jnp.exp(-z))


def _rms_to_bf16(x_ref, g_ref, xn_ref):
    tm = x_ref.shape[0]
    rc = min(tm, 256)
    for c in range(tm // rc):
        rows = slice(c * rc, (c + 1) * rc)
        x = x_ref[rows, :]
        ms = jnp.mean(x * x, axis=-1, keepdims=True)
        xn_ref[rows, :] = (x * lax.rsqrt(ms + RMS_EPS) * g_ref[...]).astype(BF16)


def _rms_matmul_kernel(x_ref, g_ref, wa_ref, wb_ref, o_ref, xn_ref, *, na):
    n = pl.program_id(1)

    @pl.when(n == 0)
    def _():
        _rms_to_bf16(x_ref, g_ref, xn_ref)

    @pl.when(n < na)
    def _():
        o_ref[...] = jnp.dot(xn_ref[...], wa_ref[...], preferred_element_type=F32)

    @pl.when(n >= na)
    def _():
        o_ref[...] = jnp.dot(xn_ref[...], wb_ref[...], preferred_element_type=F32)


def _rms_matmul(x, g, wa, wb):
    M, D = x.shape
    Na, Nb = wa.shape[1], wb.shape[1]
    tm = _pick_tile(M, (1024, 512, 256, 128, 64, 32))
    tn = _pick_tile(math.gcd(Na, Nb), (512, 256, 128))
    na, nb = Na // tn, Nb // tn
    return pl.pallas_call(
        functools.partial(_rms_matmul_kernel, na=na),
        out_shape=jax.ShapeDtypeStruct((M, Na + Nb), F32),
        grid=(M // tm, na + nb),
        in_specs=[
            pl.BlockSpec((tm, D), lambda m, n: (m, 0), pipeline_mode=pl.Buffered(1)),
            pl.BlockSpec((1, D), lambda m, n: (0, 0)),
            pl.BlockSpec((D, tn), lambda m, n: (0, jnp.minimum(n, na - 1))),
            pl.BlockSpec((D, tn), lambda m, n: (0, jnp.maximum(n - na, 0))),
        ],
        out_specs=pl.BlockSpec((tm, tn), lambda m, n: (m, n)),
        scratch_shapes=[pltpu.VMEM((tm, D), BF16)],
        compiler_params=_cparams("parallel", "arbitrary"),
        name="rms_inproj",
    )(x, g.reshape(1, D), wa, wb)


def _rwkv_prep_kernel(p_ref, prev_ref, mu_ref, w0_ref, a0_ref, kk_ref, ka_ref, rk_ref,
                      w2_ref, a2_ref, g2_ref,
                      a_out, b_out, w_out, k_out, rp_out, v_out, g_out, kr_out, bon_out,
                      last_ref, *, dr):
    @pl.when(pl.program_id(1) == 0)
    def _():
        last_ref[...] = prev_ref[...]

    p = p_ref[...]
    tb = p.shape[0]
    row = lax.broadcasted_iota(jnp.int32, p.shape, 0)
    shifted = jnp.where(row == 0, last_ref[...], pltpu.roll(p, 1, axis=0))
    last_ref[...] = p[tb - 1:tb, :]
    xm = p + (shifted - p) * mu_ref[...]

    lo = xm[:, 3 * dr:]
    wl = jnp.dot(jnp.tanh(lo).astype(BF16), w2_ref[...], preferred_element_type=F32)
    al = jnp.dot(lo.astype(BF16), a2_ref[...], preferred_element_type=F32)
    g_out[...] = jnp.dot(_sigmoid(lo).astype(BF16), g2_ref[...], preferred_element_type=F32)

    ones = _pair_ones()
    for c in range(dr // LANES):
        sl = slice(c * LANES, (c + 1) * LANES)
        r = xm[:, c * LANES:(c + 1) * LANES]
        k = xm[:, dr + c * LANES:dr + (c + 1) * LANES]
        v = xm[:, 2 * dr + c * LANES:2 * dr + (c + 1) * LANES]
        decay = jnp.exp(-jnp.exp(-_softplus(-(w0_ref[:, sl] + wl[:, sl])) - 0.5))
        asig = _sigmoid(a0_ref[:, sl] + al[:, sl])
        kk = k * kk_ref[:, sl]
        kk = kk * lax.rsqrt(jnp.maximum(_seg_sum(kk * kk, ones), 1e-24))
        k2 = k * (1.0 + (asig - 1.0) * ka_ref[:, sl])
        a_s = -kk
        b_s = kk * asig
        rp = decay * r + a_s * _seg_sum(b_s * r, ones)
        rk2 = r * k2
        a_out[:, sl] = a_s
        b_out[:, sl] = b_s
        w_out[:, sl] = decay
        k_out[:, sl] = k2
        rp_out[:, sl] = rp
        v_out[:, sl] = v
        kr_out[:, sl] = _seg_sum(rk2, ones)
        bon_out[:, sl] = _seg_sum(rk2 * rk_ref[:, sl], ones)


def _rwkv_prep(proj, shift_prev, nseq, tseq, cr, dr, mu_p, w0, a0, k_k, k_a, r_k, w2p, a2p, g2p):
    M = proj.shape[0]
    tb = _pick_tile(tseq, (128, 64, 32))
    nt = tseq // tb
    lp = cr - 3 * dr
    row = lambda a: a.reshape(1, -1).astype(F32)
    vec_spec = pl.BlockSpec((1, dr), lambda b, i: (0, 0))
    mat_spec = pl.BlockSpec((lp, dr), lambda b, i: (0, 0))
    out_spec = pl.BlockSpec((tb, dr), lambda b, i: (b * nt + i, 0))
    outs = pl.pallas_call(
        functools.partial(_rwkv_prep_kernel, dr=dr),
        out_shape=[jax.ShapeDtypeStruct((M, dr), F32)] * 9,
        grid=(nseq, nt),
        in_specs=[
            pl.BlockSpec((tb, cr), lambda b, i: (b * nt + i, 0)),
            pl.BlockSpec((None, 1, cr), lambda b, i: (b, 0, 0)),
            pl.BlockSpec((1, cr), lambda b, i: (0, 0)),
            vec_spec, vec_spec, vec_spec, vec_spec, vec_spec,
            mat_spec, mat_spec, mat_spec,
        ],
        out_specs=[out_spec] * 9,
        scratch_shapes=[pltpu.VMEM((1, cr), F32)],
        compiler_params=_cparams("parallel", "arbitrary"),
        name="rwkv_prep",
    )(proj, shift_prev, mu_p, row(w0), row(a0), row(k_k), row(k_a), row(r_k), w2p, a2p, g2p)
    return outs


def _pair_transpose(x):
    t2 = jnp.concatenate([x, x], axis=0).T
    lane = lax.broadcasted_iota(jnp.int32, x.shape, 1)
    return jnp.where(lane < RWKV_HEAD, t2[:RWKV_HEAD], t2[RWKV_HEAD:])


def _stack_heads(x):
    lane = lax.broadcasted_iota(jnp.int32, x.shape, 1)
    return jnp.concatenate([jnp.where(lane < RWKV_HEAD, x, 0.0),
                            jnp.where(lane >= RWKV_HEAD, x, 0.0)], axis=0)


def _wkv_kernel(a_ref, b_ref, w_ref, k_ref, rp_ref, v_ref, g_ref, kr_ref, bon_ref,
                s0_ref, lnw_ref, lnb_ref, y_ref, s_out_ref,
                s_scr, vt_scr, ut_scr, y0_scr, vd_scr, cf_scr, df_scr, kf_scr,
                at_scr, bt_scr, cw_scr, *, tb, npairs):
    i = pl.program_id(1)
    n = RWKV_HEAD

    @pl.when(i == 0)
    def _():
        for p in range(npairs):
            s_scr[p] = jnp.concatenate([s0_ref[2 * p], s0_ref[2 * p + 1]], axis=1)

    ones = _pair_ones()
    s_lt_j = (lax.broadcasted_iota(jnp.int32, (LANES, LANES), 0) % n
              < lax.broadcasted_iota(jnp.int32, (LANES, LANES), 1) % n)

    def pad_rows(x):
        return x if tb == n else jnp.concatenate([x, jnp.zeros((n - tb, LANES), F32)], axis=0)

    trow_all = lax.broadcasted_iota(jnp.int32, w_ref.shape, 0)
    cw = w_ref[...]
    shift = 1
    while shift < tb:
        cw = jnp.where(trow_all >= shift, cw * pltpu.roll(cw, shift, axis=0), cw)
        shift *= 2
    cw_prev = jnp.where(trow_all == 0, 1.0, pltpu.roll(cw, 1, axis=0))
    inv = 1.0 / cw
    at_scr[...] = a_ref[...] * cw_prev
    bt_scr[...] = b_ref[...] * inv
    cw_scr[...] = cw[tb - 1:tb]
    kt_all = k_ref[...] * inv
    rho_all = rp_ref[...] * cw_prev

    cols = [slice(p * LANES, (p + 1) * LANES) for p in range(npairs)]
    stacked = lambda x, sl: _stack_heads(pad_rows(x[:, sl])).astype(BF16)
    vts = [_pair_transpose(pad_rows(v_ref[:, sl])).astype(BF16) for sl in cols]
    rhos = [stacked(rho_all, sl) for sl in cols]
    afs = [stacked(at_scr, sl) for sl in cols]
    bfs = [stacked(bt_scr, sl) for sl in cols]
    kfs = [stacked(kt_all, sl) for sl in cols]
    r1s = [_dot_nt(jnp.concatenate([bfs[p], kfs[p], s_scr[p].astype(BF16)], axis=0), rhos[p])
           for p in range(npairs)]
    kas = [_dot_nt(kfs[p], afs[p]) for p in range(npairs)]
    for p in range(npairs):
        vt_scr[p] = vts[p]
        kf_scr[p] = kfs[p]
        ut_scr[p] = jnp.zeros((n, LANES), F32)
        cf_scr[p] = jnp.where(s_lt_j, r1s[p][:LANES], 0.0).astype(BF16)
        df_scr[p] = jnp.where(s_lt_j, r1s[p][LANES:2 * LANES], 0.0).astype(BF16)
        y0_scr[p] = r1s[p][2 * LANES:]
    kas = [jnp.where(s_lt_j, ka, 0.0).astype(BF16) for ka in kas]
    for p in range(npairs):
        vd_scr[p] = jnp.dot(vts[p], kas[p], preferred_element_type=F32)

    lane = lax.broadcasted_iota(jnp.int32, (n, LANES), 1) % n

    ngroups = 2 if npairs % 2 == 0 else 1
    gsize = npairs // ngroups

    def steps(t8, carry):
        rows = pl.ds(pl.multiple_of(t8 * SUBLANES, SUBLANES), SUBLANES)
        for j in range(SUBLANES):
            mask = lane == t8 * SUBLANES + j
            row = lambda ref, p: ref[rows, p * LANES:(p + 1) * LANES][j:j + 1]
            for g in range(ngroups):
                pairs = range(g * gsize, (g + 1) * gsize)
                lhs = jnp.concatenate(
                    [(s_scr[p] * row(at_scr, p) + jnp.where(mask, vd_scr[p], 0.0)).astype(BF16)
                     for p in pairs], axis=0)
                res = jnp.dot(lhs, ones, preferred_element_type=F32)
                for q, p in enumerate(pairs):
                    u = res[q * n:(q + 1) * n]
                    s_scr[p] = s_scr[p] + u * row(bt_scr, p)
                    ut_scr[p] = jnp.where(mask, u, ut_scr[p])
        return carry

    lax.fori_loop(0, tb // SUBLANES, steps, 0)

    ys = []
    for p in range(npairs):
        sl = slice(p * LANES, (p + 1) * LANES)
        vt = vt_scr[p]
        z_end = s_scr[p] + jnp.dot(vt, kf_scr[p], preferred_element_type=F32)
        s_scr[p] = z_end * cw_scr[:, sl]
        yt = (y0_scr[p]
              + jnp.dot(ut_scr[p].astype(BF16), cf_scr[p], preferred_element_type=F32)
              + jnp.dot(vt, df_scr[p], preferred_element_type=F32))
        ys.append(_pair_transpose(yt)[:tb])
    stack = lambda ref: jnp.concatenate(
        [ref[:, p * LANES:(p + 1) * LANES] for p in range(npairs)], axis=0)
    v_all = stack(v_ref)
    y = jnp.concatenate(ys, axis=0) + v_all * stack(kr_ref)
    inv_n = 1.0 / n
    mean = _seg_sum(y, ones) * inv_n
    d = y - mean
    var = _seg_sum(d * d, ones) * inv_n
    yn = d * lax.rsqrt(var + GN_EPS)
    bonus = stack(bon_ref) * v_all
    for p in range(npairs):
        sl = slice(p * LANES, (p + 1) * LANES)
        rows_p = slice(p * tb, (p + 1) * tb)
        out = (yn[rows_p] * lnw_ref[:, sl] + lnb_ref[:, sl] + bonus[rows_p]) * g_ref[:, sl]
        y_ref[:, sl] = out.astype(BF16)

    @pl.when(i == pl.num_programs(1) - 1)
    def _():
        for p in range(npairs):
            s = s_scr[p]
            s_out_ref[2 * p] = s[:, :n]
            s_out_ref[2 * p + 1] = s[:, n:]


def _wkv(prep, s0, nseq, tseq, dr, ln_w, ln_b):
    M = prep[0].shape[0]
    tb = min(tseq, RWKV_HEAD)
    nt = tseq // tb
    npairs = dr // LANES
    assert npairs % 2 == 0
    blk = pl.BlockSpec((tb, dr), lambda b, i: (b * nt + i, 0))
    st = pl.BlockSpec((None, 2 * npairs, RWKV_HEAD, RWKV_HEAD), lambda b, i: (b, 0, 0, 0))
    vec = pl.BlockSpec((1, dr), lambda b, i: (0, 0))
    return pl.pallas_call(
        functools.partial(_wkv_kernel, tb=tb, npairs=npairs),
        out_shape=[jax.ShapeDtypeStruct((M, dr), BF16),
                   jax.ShapeDtypeStruct(s0.shape, F32)],
        grid=(nseq, nt),
        in_specs=[blk] * 9 + [st, vec, vec],
        out_specs=[blk, st],
        scratch_shapes=[pltpu.VMEM((npairs, RWKV_HEAD, LANES), F32),
                        pltpu.VMEM((npairs, RWKV_HEAD, LANES), BF16),
                        pltpu.VMEM((npairs, RWKV_HEAD, LANES), F32),
                        pltpu.VMEM((npairs, RWKV_HEAD, LANES), F32),
                        pltpu.VMEM((npairs, RWKV_HEAD, LANES), F32),
                        pltpu.VMEM((npairs, LANES, LANES), BF16),
                        pltpu.VMEM((npairs, LANES, LANES), BF16),
                        pltpu.VMEM((npairs, LANES, LANES), BF16),
                        pltpu.VMEM((tb, dr), F32), pltpu.VMEM((tb, dr), F32),
                        pltpu.VMEM((1, dr), F32)],
        compiler_params=_cparams("parallel", "arbitrary"),
        name="wkv7",
    )(*prep, s0, ln_w.reshape(1, dr).astype(F32), ln_b.reshape(1, dr).astype(F32))


def _bias_kernel(tab_ref, band_ref, dec_ref, *, nrel, qb, dec_t, dec_w):
    h = pl.program_id(0)
    m = lax.broadcasted_iota(jnp.int32, (SUBLANES, TOEP_W), 1)
    dd = jnp.where(m < TOEP_W // 2, m, m - TOEP_W)
    idx = jnp.clip(dd - ATT_CTX, -REL_CLIP, REL_CLIP) + REL_CLIP

    def pick(r, acc):
        return jnp.where(idx == r, tab_ref[h * nrel + r], acc)

    e = lax.fori_loop(0, nrel, pick, jnp.zeros((SUBLANES, TOEP_W), F32))
    eb = jnp.broadcast_to(e[0:1], (qb, TOEP_W))
    toe = pltpu.roll(eb, 0, axis=1, stride=1, stride_axis=0)
    kb = band_ref.shape[-1]
    cq = lax.broadcasted_iota(jnp.int32, (qb, kb), 0) // CHUNK
    ck = lax.broadcasted_iota(jnp.int32, (qb, kb), 1) // CHUNK
    valid = (ck >= cq) & (ck <= cq + N_PAST_CHUNKS)
    band_ref[...] = jnp.where(valid, toe[:, :kb], NEG_INF)
    dec_ref[...] = toe[:dec_t, :dec_w]


def _bias_tiles(table, qb, dec_t, dec_w):
    nrel, H = table.shape
    tab = table.T.reshape(-1).astype(F32)
    return pl.pallas_call(
        functools.partial(_bias_kernel, nrel=nrel, qb=qb, dec_t=dec_t, dec_w=dec_w),
        out_shape=[jax.ShapeDtypeStruct((H, qb, qb + ATT_CTX), F32),
                   jax.ShapeDtypeStruct((H, dec_t, dec_w), F32)],
        grid=(H,),
        in_specs=[pl.BlockSpec(memory_space=pltpu.SMEM)],
        out_specs=[pl.BlockSpec((None, qb, qb + ATT_CTX), lambda h: (h, 0, 0)),
                   pl.BlockSpec((None, dec_t, dec_w), lambda h: (h, 0, 0))],
        compiler_params=_cparams("arbitrary"),
        name="rel_bias_tiles",
    )(tab)


def _head_rms(x, g):
    return x * lax.rsqrt(jnp.mean(x * x, axis=-1, keepdims=True) + RMS_EPS) * g


def _dot_nt(a, b):
    return lax.dot_general(a, b, (((1,), (1,)), ((), ())), preferred_element_type=F32)


def _heads_per_step(nheads, col0, most):
    for hps in (4, 2, 1):
        if hps <= most and nheads % hps == 0 and col0 % (hps * ATT_HEAD) == 0:
            return hps
    raise ValueError("attention columns are not head aligned")


def _softmax_pv(s_a, s_b, v_a, v_b):
    m = jnp.maximum(jnp.max(s_a, axis=-1, keepdims=True), jnp.max(s_b, axis=-1, keepdims=True))
    p_a = jnp.exp(s_a - m)
    p_b = jnp.exp(s_b - m)
    l = jnp.sum(p_a, axis=-1, keepdims=True) + jnp.sum(p_b, axis=-1, keepdims=True)
    o = (jnp.dot(p_a.astype(BF16), v_a.astype(BF16), preferred_element_type=F32)
         + jnp.dot(p_b.astype(BF16), v_b.astype(BF16), preferred_element_type=F32))
    return o / l


def _band_attn_kernel(q_ref, kc_ref, kp_ref, vc_ref, vp_ref, qg_ref, kg_ref, bias_ref,
                      o_ref, knew_ref, *, hps):
    i = pl.program_id(2)
    scale = ATT_HEAD ** -0.5
    for hh in range(hps):
        sl = slice(hh * ATT_HEAD, (hh + 1) * ATT_HEAD)
        qn = _head_rms(q_ref[:, sl], qg_ref[...]).astype(BF16)
        kc = _head_rms(kc_ref[:, sl], kg_ref[...])
        kp = _head_rms(kp_ref[:, sl], kg_ref[...])
        s_p = _dot_nt(qn, kp.astype(BF16)) * scale + bias_ref[hh, :, :ATT_CTX]
        s_p = jnp.where(i > 0, s_p, NEG_INF)
        s_c = _dot_nt(qn, kc.astype(BF16)) * scale + bias_ref[hh, :, ATT_CTX:]
        o_ref[:, sl] = _softmax_pv(s_p, s_c, vp_ref[:, sl], vc_ref[:, sl]).astype(BF16)

        @pl.when(i == pl.num_programs(2) - 1)
        def _():
            knew_ref[:, sl] = kc


def _band_attention(proj, col0, nseq, tseq, nheads, q_g, k_g, bias_band):
    M = proj.shape[0]
    qb = ATT_CTX
    assert tseq % qb == 0
    nb = tseq // qb
    H = nheads
    hps = _heads_per_step(H, col0, 2)
    w = hps * ATT_HEAD
    c0 = col0 // w
    hb = H // hps
    cur = lambda off: pl.BlockSpec((qb, w), lambda b, h, i: (b * nb + i, c0 + off + h))
    prv = lambda off: pl.BlockSpec(
        (qb, w), lambda b, h, i: (b * nb + jnp.maximum(i - 1, 0), c0 + off + h))
    gspec = pl.BlockSpec((1, ATT_HEAD), lambda b, h, i: (0, 0))
    return pl.pallas_call(
        functools.partial(_band_attn_kernel, hps=hps),
        out_shape=[jax.ShapeDtypeStruct((M, H * ATT_HEAD), BF16),
                   jax.ShapeDtypeStruct((nseq, qb, H * ATT_HEAD), F32)],
        grid=(nseq, hb, nb),
        in_specs=[cur(0), cur(hb), prv(hb), cur(2 * hb), prv(2 * hb), gspec, gspec,
                  pl.BlockSpec((hps, qb, qb + ATT_CTX), lambda b, h, i: (h, 0, 0))],
        out_specs=[pl.BlockSpec((qb, w), lambda b, h, i: (b * nb + i, h)),
                   pl.BlockSpec((None, qb, w), lambda b, h, i: (b, 0, h))],
        compiler_params=_cparams("parallel", "parallel", "arbitrary"),
        name="band_attention",
    )(proj, proj, proj, proj, proj, q_g.reshape(1, -1), k_g.reshape(1, -1), bias_band)


def _cached_attn_kernel(q_ref, kn_ref, vn_ref, kc_ref, vc_ref, qg_ref, kg_ref, bias_ref,
                        o_ref, knew_ref, *, t, hps, nheads):
    scale = ATT_HEAD ** -0.5
    L = kc_ref.shape[0] // nheads
    h0 = pl.program_id(1) * hps
    for h in range(hps):
        sl = slice(h * ATT_HEAD, (h + 1) * ATT_HEAD)
        rows = pl.ds(h0 + h, L, stride=nheads)
        qn = _head_rms(q_ref[:, sl], qg_ref[...]).astype(BF16)
        kn = _head_rms(kn_ref[:, sl], kg_ref[...])
        knew_ref[:, sl] = kn
        s_c = _dot_nt(qn, kc_ref[rows, :].astype(BF16)) * scale + bias_ref[h, :, :L]
        s_n = _dot_nt(qn, kn.astype(BF16)) * scale + bias_ref[h, :, L:L + t]
        o_ref[:, sl] = _softmax_pv(s_c, s_n, vc_ref[rows, :], vn_ref[:, sl]).astype(BF16)


def _cached_attention(proj, col0, nseq, tseq, nheads, k_cache, v_cache, q_g, k_g, bias_dec):
    M = proj.shape[0]
    H = nheads
    L = k_cache.shape[1] // H
    da = H * ATT_HEAD
    hps = _heads_per_step(H, col0, 4)
    w = hps * ATT_HEAD
    c0 = col0 // w
    hb = H // hps
    new = lambda off: pl.BlockSpec((tseq, w), lambda b, h: (b, c0 + off + h))
    cache = pl.BlockSpec((None, L * H, ATT_HEAD), lambda b, h: (b, 0, 0))
    gspec = pl.BlockSpec((1, ATT_HEAD), lambda b, h: (0, 0))
    out = pl.BlockSpec((tseq, w), lambda b, h: (b, h))
    return pl.pallas_call(
        functools.partial(_cached_attn_kernel, t=tseq, hps=hps, nheads=H),
        out_shape=[jax.ShapeDtypeStruct((M, da), BF16),
                   jax.ShapeDtypeStruct((M, da), F32)],
        grid=(nseq, hb),
        in_specs=[new(0), new(hb), new(2 * hb), cache, cache, gspec, gspec,
                  pl.BlockSpec((hps,) + bias_dec.shape[1:], lambda b, h: (h, 0, 0))],
        out_specs=[out, out],
        compiler_params=_cparams("parallel", "arbitrary"),
        name="cached_attention",
    )(proj, proj, proj, k_cache, v_cache, q_g.reshape(1, -1), k_g.reshape(1, -1), bias_dec)


def _outproj_kernel(yr_ref, ya_ref, w_ref, x_ref, h_ref, *, dr):
    acc = jnp.dot(yr_ref[...], w_ref[:dr, :], preferred_element_type=F32)
    acc += jnp.dot(ya_ref[...], w_ref[dr:, :], preferred_element_type=F32)
    h_ref[...] = x_ref[...] + acc


def _outproj(y_r, y_a, w_out, x):
    M, D = x.shape
    dr, da = y_r.shape[1], y_a.shape[1]
    tm = _pick_tile(M, (1024, 512, 256, 128, 64, 32))
    tn = _pick_tile(D, (512, 256, 128))
    return pl.pallas_call(
        functools.partial(_outproj_kernel, dr=dr),
        out_shape=jax.ShapeDtypeStruct((M, D), F32),
        grid=(M // tm, D // tn),
        in_specs=[
            pl.BlockSpec((tm, dr), lambda m, n: (m, 0)),
            pl.BlockSpec((tm, da), lambda m, n: (m, 0)),
            pl.BlockSpec((dr + da, tn), lambda m, n: (0, n)),
            pl.BlockSpec((tm, tn), lambda m, n: (m, n)),
        ],
        out_specs=pl.BlockSpec((tm, tn), lambda m, n: (m, n)),
        compiler_params=_cparams("parallel", "arbitrary"),
        name="out_proj",
    )(y_r, y_a, w_out, x)


def _ffn_up_kernel(h_ref, g_ref, wg_ref, wv_ref, cw_ref, cb_ref, b1_ref, b2_ref,
                   act_ref, gate_ref, hn_ref, carry_ref, *, tseq, blocks_per_seq, nsplit):
    m = pl.program_id(0)
    n = pl.program_id(1)

    @pl.when(n == 0)
    def _():
        _rms_to_bf16(h_ref, g_ref, hn_ref)

    tm = hn_ref.shape[0]
    hs = tm // nsplit
    tn = wg_ref.shape[1]
    row8 = lax.broadcasted_iota(jnp.int32, (SUBLANES, tn), 0)
    gate = None
    for c in range(nsplit):
        rows = slice(c * hs, (c + 1) * hs)
        hn = hn_ref[rows, :]
        prev_gate = gate
        gate = jnp.dot(hn, wg_ref[...], preferred_element_type=F32)
        val = jnp.dot(hn, wv_ref[...], preferred_element_type=F32)
        r1 = pltpu.roll(gate, 1, axis=0)
        r2 = pltpu.roll(gate, 2, axis=0)
        if blocks_per_seq >= 1:
            if c == 0:
                first = m % blocks_per_seq == 0
                prev = carry_ref[n]
                p2 = jnp.where(first, b2_ref[0:1, :], prev[SUBLANES - 2:SUBLANES - 1, :])
                p1 = jnp.where(first, b1_ref[0:1, :], prev[SUBLANES - 1:SUBLANES, :])
            else:
                p2 = prev_gate[hs - 2:hs - 1, :]
                p1 = prev_gate[hs - 1:hs, :]
            top1 = jnp.where(row8 == 0, p1, r1[:SUBLANES])
            top2 = jnp.where(row8 == 0, p2, jnp.where(row8 == 1, p1, r2[:SUBLANES]))
            g1 = jnp.concatenate([top1, r1[SUBLANES:]], axis=0)
            g2 = jnp.concatenate([top2, r2[SUBLANES:]], axis=0)
        else:
            t = lax.broadcasted_iota(jnp.int32, gate.shape, 0) % tseq
            g1 = jnp.where(t == 0, b1_ref[rows, :], r1)
            g2 = jnp.where(t < 2, b2_ref[rows, :], r2)
            gate_ref[rows, :] = gate
        conv = cb_ref[...] + g2 * cw_ref[0:1, :] + g1 * cw_ref[1:2, :] + gate * cw_ref[2:3, :]
        act_ref[rows, :] = (conv * _sigmoid(conv) * val).astype(BF16)
    if blocks_per_seq >= 1:
        carry_ref[n] = gate[hs - SUBLANES:, :]
        gate_ref[...] = gate[hs - (FFN_CONV - 1):, :]


def _ffn_up(h, g, w_up, conv_w, conv_b, conv_prev, nseq, tseq):
    M, D = h.shape
    Fd = w_up.shape[1] // 2
    tn = _pick_tile(Fd, (256, 128))
    nn = Fd // tn
    nsplit = 2
    if tseq >= 512:
        tm = _pick_tile(tseq, (1024, 512))
        bps = tseq // tm
        b1 = conv_prev[:, 1:2, :]
        b2 = conv_prev[:, 0:1, :]
        bspec = pl.BlockSpec((None, 1, tn), lambda m, n: (m // bps, 0, n))
        gate_shape = jax.ShapeDtypeStruct((M // tm, FFN_CONV - 1, Fd), F32)
        gate_spec = pl.BlockSpec((None, FFN_CONV - 1, tn), lambda m, n: (m, 0, n))
    else:
        tm = M
        bps = 0
        z = jnp.zeros((nseq, tseq, Fd), F32)
        b1 = z.at[:, 0].set(conv_prev[:, 1]).reshape(M, Fd)
        b2 = z.at[:, 0].set(conv_prev[:, 0]).at[:, 1].set(conv_prev[:, 1]).reshape(M, Fd)
        bspec = pl.BlockSpec((tm, tn), lambda m, n: (m, n))
        gate_shape = jax.ShapeDtypeStruct((M, Fd), F32)
        gate_spec = pl.BlockSpec((tm, tn), lambda m, n: (m, n))
    act, gate = pl.pallas_call(
        functools.partial(_ffn_up_kernel, tseq=tseq, blocks_per_seq=bps, nsplit=nsplit),
        out_shape=[jax.ShapeDtypeStruct((M, Fd), BF16), gate_shape],
        grid=(M // tm, nn),
        in_specs=[
            pl.BlockSpec((tm, D), lambda m, n: (m, 0), pipeline_mode=pl.Buffered(1)),
            pl.BlockSpec((1, D), lambda m, n: (0, 0)),
            pl.BlockSpec((D, tn), lambda m, n: (0, n)),
            pl.BlockSpec((D, tn), lambda m, n: (0, nn + n)),
            pl.BlockSpec((FFN_CONV, tn), lambda m, n: (0, n)),
            pl.BlockSpec((1, tn), lambda m, n: (0, n)),
            bspec, bspec,
        ],
        out_specs=[pl.BlockSpec((tm, tn), lambda m, n: (m, n)), gate_spec],
        scratch_shapes=[pltpu.VMEM((tm, D), BF16), pltpu.VMEM((nn, SUBLANES, tn), F32)],
        compiler_params=_cparams("arbitrary", "arbitrary"),
        name="ffn_up",
    )(h, g.reshape(1, D), w_up, w_up, conv_w, conv_b.reshape(1, Fd), b1, b2)
    if bps == 0:
        gate = gate.reshape(nseq, tseq, Fd)[:, tseq - (FFN_CONV - 1):]
    else:
        gate = gate[bps - 1::bps]
    return act, gate


def _ffn_down_kernel(a_ref, w_ref, h_ref, o_ref):
    o_ref[...] = h_ref[...] + jnp.dot(a_ref[...], w_ref[...], preferred_element_type=F32)


def _ffn_down(act, w_down, h):
    M, Fd = act.shape
    D = w_down.shape[1]
    tm = _pick_tile(M, (512, 256, 128, 64, 32))
    tn = _pick_tile(D, (256, 128))
    return pl.pallas_call(
        _ffn_down_kernel,
        out_shape=jax.ShapeDtypeStruct((M, D), F32),
        grid=(M // tm, D // tn),
        in_specs=[
            pl.BlockSpec((tm, Fd), lambda m, n: (m, 0)),
            pl.BlockSpec((Fd, tn), lambda m, n: (0, n)),
            pl.BlockSpec((tm, tn), lambda m, n: (m, n)),
        ],
        out_specs=pl.BlockSpec((tm, tn), lambda m, n: (m, n)),
        compiler_params=_cparams("parallel", "arbitrary"),
        name="ffn_down",
    )(act, w_down, h)


def _layer_weights(lp):
    (attn_norm_g, w_in, mu, w0, w2, a0, a2, g2, k_k, k_a, r_k, ln_w, ln_b, q_g, k_g, rel_bias,
     w_out, ffn_norm_g, w_up, conv_w, conv_b, w_down) = lp
    D = w_in.shape[0]
    dr = w0.shape[0]
    dl, al, gl = w2.shape[0], a2.shape[0], g2.shape[0]
    nrc = 3 * dr + dl + al + gl
    lpad = _round_up(dl + al + gl, LANES)
    cr = 3 * dr + lpad
    assert cr <= w_in.shape[1]
    w_in_r = w_in[:, :cr].astype(BF16)
    w_in_a = w_in[:, nrc:].astype(BF16)
    mu_p = jnp.concatenate([mu, jnp.zeros((cr - nrc,), mu.dtype)]).reshape(1, cr)
    zrow = lambda n: jnp.zeros((n, dr), F32)
    w2p = jnp.concatenate([w2, zrow(lpad - dl)], axis=0).astype(BF16)
    a2p = jnp.concatenate([zrow(dl), a2, zrow(lpad - dl - al)], axis=0).astype(BF16)
    g2p = jnp.concatenate([zrow(dl + al), g2, zrow(lpad - dl - al - gl)], axis=0).astype(BF16)
    return dict(attn_norm_g=attn_norm_g, w_in_r=w_in_r, w_in_a=w_in_a, mu_p=mu_p, w0=w0, a0=a0, k_k=k_k, k_a=k_a,
                r_k=r_k.reshape(-1), w2p=w2p, a2p=a2p, g2p=g2p, ln_w=ln_w, ln_b=ln_b, q_g=q_g, k_g=k_g,
                rel_bias=rel_bias, w_out=w_out.astype(BF16), ffn_norm_g=ffn_norm_g,
                w_up=w_up.astype(BF16), conv_w=conv_w, conv_b=conv_b, w_down=w_down.astype(BF16),
                nrc=nrc, cr=cr, dr=dr)


def _block(x, shift_prev, wkv_prev, conv_prev, k_cache, v_cache, W, bias_band, bias_dec):
    B, T, D = x.shape
    M = B * T
    dr, cr, nrc = W["dr"], W["cr"], W["nrc"]
    H = (D - dr) // ATT_HEAD
    xf = x.reshape(M, D)
    proj = _rms_matmul(xf, W["attn_norm_g"], W["w_in_r"], W["w_in_a"])

    shift_p = jnp.concatenate([shift_prev, jnp.zeros((B, 1, cr - nrc), F32)], axis=-1)
    prep = _rwkv_prep(proj, shift_p, B, T, cr, dr, W["mu_p"], W["w0"], W["a0"], W["k_k"], W["k_a"],
                      W["r_k"], W["w2p"], W["a2p"], W["g2p"])
    y_r, wkv_new = _wkv(prep, wkv_prev, B, T, dr, W["ln_w"], W["ln_b"])
    shift_new = proj.reshape(B, T, -1)[:, T - 1:, :nrc]

    da = H * ATT_HEAD
    if k_cache is None:
        y_a, k_new = _band_attention(proj, cr, B, T, H, W["q_g"], W["k_g"], bias_band)
        keep = min(ATT_CTX, T)
        k_new = k_new.reshape(B, keep, H, ATT_HEAD)
        v_new = proj.reshape(B, T, -1)[:, T - keep:, cr + 2 * da:].reshape(B, keep, H, ATT_HEAD)
    else:
        L = k_cache.shape[1]
        y_a, k_new = _cached_attention(proj, cr, B, T, H, k_cache.reshape(B, L * H, ATT_HEAD),
                                       v_cache.reshape(B, L * H, ATT_HEAD), W["q_g"], W["k_g"], bias_dec)
        k_new = k_new.reshape(B, T, H, ATT_HEAD)
        v_new = proj.reshape(B, T, -1)[:, :, cr + 2 * da:].reshape(B, T, H, ATT_HEAD)

    h = _outproj(y_r, y_a, W["w_out"], xf)
    act, conv_new = _ffn_up(h, W["ffn_norm_g"], W["w_up"], W["conv_w"], W["conv_b"], conv_prev, B, T)
    y = _ffn_down(act, W["w_down"], h)
    return y.reshape(B, T, D), shift_new, wkv_new, k_new, v_new, conv_new


def kernel(x_prompt, x_sample, state_rwkv_shift, state_rwkv_wkv, cache_attn_k, cache_attn_v,
           state_ffn_conv, attn_norm_g, w_in, rwkv_mu, rwkv_w0, rwkv_w2, rwkv_a0, rwkv_a2, rwkv_g2,
           rwkv_k_k, rwkv_k_a, rwkv_r_k, rwkv_ln_w, rwkv_ln_b, att_q_norm_g, att_k_norm_g,
           att_rel_bias, w_out, ffn_norm_g, ffn_w_up, ffn_conv_w, ffn_conv_b, ffn_w_down):
    depth = w_in.shape[0]
    B = x_prompt.shape[0]
    xp, xs = x_prompt, x_sample
    dec_t = x_sample.shape[1]
    L = cache_attn_k.shape[2]
    assert L == ATT_CTX
    dec_w = _round_up(L + dec_t, LANES)
    p_states = ([], [], [], [], [])
    s_states = ([], [], [], [], [])
    for l in range(depth):
        lp = (attn_norm_g[l], w_in[l], rwkv_mu[l], rwkv_w0[l], rwkv_w2[l], rwkv_a0[l], rwkv_a2[l],
              rwkv_g2[l], rwkv_k_k[l], rwkv_k_a[l], rwkv_r_k[l], rwkv_ln_w[l], rwkv_ln_b[l],
              att_q_norm_g[l], att_k_norm_g[l], att_rel_bias[l], w_out[l], ffn_norm_g[l],
              ffn_w_up[l], ffn_conv_w[l], ffn_conv_b[l], ffn_w_down[l])
        W = _layer_weights(lp)
        nrc, dr = W["nrc"], W["dr"]
        Fd = ffn_conv_b.shape[-1]
        bias_band, bias_dec = _bias_tiles(att_rel_bias[l], ATT_CTX, dec_t, dec_w)
        shift0 = jnp.zeros((B, 1, nrc), F32)
        wkv0 = jnp.zeros((B, dr // RWKV_HEAD, RWKV_HEAD, RWKV_HEAD), F32)
        conv0 = jnp.zeros((B, FFN_CONV - 1, Fd), F32)
        xp, *ps = _block(xp, shift0, wkv0, conv0, None, None, W, bias_band, bias_dec)
        xs, *ss = _block(xs, state_rwkv_shift[l], state_rwkv_wkv[l], state_ffn_conv[l],
                         cache_attn_k[l], cache_attn_v[l], W, bias_band, bias_dec)
        for i in range(5):
            p_states[i].append(ps[i])
            s_states[i].append(ss[i])
    p_out = [jnp.stack(t, axis=0) for t in p_states]
    s_out = [jnp.stack(t, axis=0) for t in s_states]
    return (xp, xs, *p_out, *s_out)
```

```python
import functools
import math

import jax
import jax.numpy as jnp
from jax import lax
from jax.experimental import pallas as pl
from jax.experimental.pallas import tpu as pltpu

F32 = jnp.float32
BF16 = jnp.bfloat16

RMS_EPS = 1e-6
GN_EPS = 64e-5
NEG_INF = -1e30
CHUNK = 64
N_PAST_CHUNKS = 8
ATT_CTX = N_PAST_CHUNKS * CHUNK
REL_CLIP = 128
RWKV_HEAD = 64
ATT_HEAD = 128
FFN_CONV = 3

LANES = 128
SUBLANES = 8
VMEM_LIMIT = 56 * 1024 * 1024
TOEP_W = 2048


def _cparams(*sem):
    return pltpu.CompilerParams(dimension_semantics=sem, vmem_limit_bytes=VMEM_LIMIT)


def _round_up(n, m):
    return (n + m - 1) // m * m


def _pick_tile(n, candidates):
    for c in candidates:
        if n % c == 0:
            return c
    raise ValueError(f"no tile in {candidates} divides {n}")


def _pair_ones():
    r = lax.broadcasted_iota(jnp.int32, (LANES, LANES), 0) // RWKV_HEAD
    c = lax.broadcasted_iota(jnp.int32, (LANES, LANES), 1) // RWKV_HEAD
    return (r == c).astype(BF16)


def _seg_sum(x, ones):
    hi = x.astype(BF16)
    lo = (x - hi.astype(F32)).astype(BF16)
    return (jnp.dot(hi, ones, preferred_element_type=F32)
            + jnp.dot(lo, ones, preferred_element_type=F32))


def _softplus(z):
    return jnp.maximum(z, 0.0) + jnp.log1p(jnp.exp(-jnp.abs(z)))


def _sigmoid(z):
    return 1.0 / (1.0 + jnp.exp(-z))


def _rms_to_bf16(x_ref, g_ref, xn_ref):
    tm = x_ref.shape[0]
    rc = min(tm, 256)
    for c in range(tm // rc):
        rows = slice(c * rc, (c + 1) * rc)
        x = x_ref[rows, :]
        ms = jnp.mean(x * x, axis=-1, keepdims=True)
        xn_ref[rows, :] = (x * lax.rsqrt(ms + RMS_EPS) * g_ref[...]).astype(BF16)


def _rms_matmul_kernel(x_ref, g_ref, wa_ref, wb_ref, o_ref, xn_ref, *, na):
    n = pl.program_id(1)

    @pl.when(n == 0)
    def _():
        _rms_to_bf16(x_ref, g_ref, xn_ref)

    @pl.when(n < na)
    def _():
        o_ref[...] = jnp.dot(xn_ref[...], wa_ref[...], preferred_element_type=F32)

    @pl.when(n >= na)
    def _():
        o_ref[...] = jnp.dot(xn_ref[...], wb_ref[...], preferred_element_type=F32)


def _rms_matmul(x, g, wa, wb):
    M, D = x.shape
    Na, Nb = wa.shape[1], wb.shape[1]
    tm = _pick_tile(M, (1024, 512, 256, 128, 64, 32))
    tn = _pick_tile(math.gcd(Na, Nb), (512, 256, 128))
    na, nb = Na // tn, Nb // tn
    return pl.pallas_call(
        functools.partial(_rms_matmul_kernel, na=na),
        out_shape=jax.ShapeDtypeStruct((M, Na + Nb), F32),
        grid=(M // tm, na + nb),
        in_specs=[
            pl.BlockSpec((tm, D), lambda m, n: (m, 0), pipeline_mode=pl.Buffered(1)),
            pl.BlockSpec((1, D), lambda m, n: (0, 0)),
            pl.BlockSpec((D, tn), lambda m, n: (0, jnp.minimum(n, na - 1))),
            pl.BlockSpec((D, tn), lambda m, n: (0, jnp.maximum(n - na, 0))),
        ],
        out_specs=pl.BlockSpec((tm, tn), lambda m, n: (m, n)),
        scratch_shapes=[pltpu.VMEM((tm, D), BF16)],
        compiler_params=_cparams("parallel", "arbitrary"),
        name="rms_inproj",
    )(x, g.reshape(1, D), wa, wb)


def _rwkv_prep_kernel(p_ref, prev_ref, mu_ref, w0_ref, a0_ref, kk_ref, ka_ref, rk_ref,
                      w2_ref, a2_ref, g2_ref,
                      a_out, b_out, w_out, k_out, rp_out, v_out, g_out, kr_out, bon_out,
                      last_ref, *, dr):
    @pl.when(pl.program_id(1) == 0)
    def _():
        last_ref[...] = prev_ref[...]

    p = p_ref[...]
    tb = p.shape[0]
    row = lax.broadcasted_iota(jnp.int32, p.shape, 0)
    shifted = jnp.where(row == 0, last_ref[...], pltpu.roll(p, 1, axis=0))
    last_ref[...] = p[tb - 1:tb, :]
    xm = p + (shifted - p) * mu_ref[...]

    lo = xm[:, 3 * dr:]
    wl = jnp.dot(jnp.tanh(lo).astype(BF16), w2_ref[...], preferred_element_type=F32)
    al = jnp.dot(lo.astype(BF16), a2_ref[...], preferred_element_type=F32)
    g_out[...] = jnp.dot(_sigmoid(lo).astype(BF16), g2_ref[...], preferred_element_type=F32)

    ones = _pair_ones()
    for c in range(dr // LANES):
        sl = slice(c * LANES, (c + 1) * LANES)
        r = xm[:, c * LANES:(c + 1) * LANES]
        k = xm[:, dr + c * LANES:dr + (c + 1) * LANES]
        v = xm[:, 2 * dr + c * LANES:2 * dr + (c + 1) * LANES]
        decay = jnp.exp(-jnp.exp(-_softplus(-(w0_ref[:, sl] + wl[:, sl])) - 0.5))
        asig = _sigmoid(a0_ref[:, sl] + al[:, sl])
        kk = k * kk_ref[:, sl]
        kk = kk * lax.rsqrt(jnp.maximum(_seg_sum(kk * kk, ones), 1e-24))
        k2 = k * (1.0 + (asig - 1.0) * ka_ref[:, sl])
        a_s = -kk
        b_s = kk * asig
        rp = decay * r + a_s * _seg_sum(b_s * r, ones)
        rk2 = r * k2
        a_out[:, sl] = a_s
        b_out[:, sl] = b_s
        w_out[:, sl] = decay
        k_out[:, sl] = k2
        rp_out[:, sl] = rp
        v_out[:, sl] = v
        kr_out[:, sl] = _seg_sum(rk2, ones)
        bon_out[:, sl] = _seg_sum(rk2 * rk_ref[:, sl], ones)


def _rwkv_prep(proj, shift_prev, nseq, tseq, cr, dr, mu_p, w0, a0, k_k, k_a, r_k, w2p, a2p, g2p):
    M = proj.shape[0]
    tb = _pick_tile(tseq, (128, 64, 32))
    nt = tseq // tb
    lp = cr - 3 * dr
    row = lambda a: a.reshape(1, -1).astype(F32)
    vec_spec = pl.BlockSpec((1, dr), lambda b, i: (0, 0))
    mat_spec = pl.BlockSpec((lp, dr), lambda b, i: (0, 0))
    out_spec = pl.BlockSpec((tb, dr), lambda b, i: (b * nt + i, 0))
    outs = pl.pallas_call(
        functools.partial(_rwkv_prep_kernel, dr=dr),
        out_shape=[jax.ShapeDtypeStruct((M, dr), F32)] * 9,
        grid=(nseq, nt),
        in_specs=[
            pl.BlockSpec((tb, cr), lambda b, i: (b * nt + i, 0)),
            pl.BlockSpec((None, 1, cr), lambda b, i: (b, 0, 0)),
            pl.BlockSpec((1, cr), lambda b, i: (0, 0)),
            vec_spec, vec_spec, vec_spec, vec_spec, vec_spec,
            mat_spec, mat_spec, mat_spec,
        ],
        out_specs=[out_spec] * 9,
        scratch_shapes=[pltpu.VMEM((1, cr), F32)],
        compiler_params=_cparams("parallel", "arbitrary"),
        name="rwkv_prep",
    )(proj, shift_prev, mu_p, row(w0), row(a0), row(k_k), row(k_a), row(r_k), w2p, a2p, g2p)
    return outs


def _pair_transpose(x):
    t2 = jnp.concatenate([x, x], axis=0).T
    lane = lax.broadcasted_iota(jnp.int32, x.shape, 1)
    return jnp.where(lane < RWKV_HEAD, t2[:RWKV_HEAD], t2[RWKV_HEAD:])


def _stack_heads(x):
    lane = lax.broadcasted_iota(jnp.int32, x.shape, 1)
    return jnp.concatenate([jnp.where(lane < RWKV_HEAD, x, 0.0),
                            jnp.where(lane >= RWKV_HEAD, x, 0.0)], axis=0)


def _wkv_kernel(a_ref, b_ref, w_ref, k_ref, rp_ref, v_ref, g_ref, kr_ref, bon_ref,
                s0_ref, lnw_ref, lnb_ref, y_ref, s_out_ref,
                s_scr, vt_scr, ut_scr, y0_scr, vd_scr, cf_scr, df_scr, kf_scr,
                at_scr, bt_scr, cw_scr, *, tb, npairs, nb):
    i = pl.program_id(1)
    n = RWKV_HEAD

    units = [(bi, p) for bi in range(nb) for p in range(npairs)]
    nu = len(units)
    col = lambda p: slice(p * LANES, (p + 1) * LANES)

    @pl.when(i == 0)
    def _():
        for q, (bi, p) in enumerate(units):
            s_scr[q] = jnp.concatenate([s0_ref[bi, 2 * p], s0_ref[bi, 2 * p + 1]], axis=1)

    ones = _pair_ones()
    s_lt_j = (lax.broadcasted_iota(jnp.int32, (LANES, LANES), 0) % n
              < lax.broadcasted_iota(jnp.int32, (LANES, LANES), 1) % n)

    def pad_rows(x):
        return x if tb == n else jnp.concatenate([x, jnp.zeros((n - tb, LANES), F32)], axis=0)

    trow_all = lax.broadcasted_iota(jnp.int32, w_ref.shape[1:], 0)
    kt_all, rho_all = [], []
    for bi in range(nb):
        cw = w_ref[bi]
        shift = 1
        while shift < tb:
            cw = jnp.where(trow_all >= shift, cw * pltpu.roll(cw, shift, axis=0), cw)
            shift *= 2
        cw_prev = jnp.where(trow_all == 0, 1.0, pltpu.roll(cw, 1, axis=0))
        inv = 1.0 / cw
        at_scr[bi] = a_ref[bi] * cw_prev
        bt_scr[bi] = b_ref[bi] * inv
        cw_scr[bi] = cw[tb - 1:tb]
        kt_all.append(k_ref[bi] * inv)
        rho_all.append(rp_ref[bi] * cw_prev)

    stacked = lambda x: _stack_heads(pad_rows(x)).astype(BF16)
    vts = [_pair_transpose(pad_rows(v_ref[bi, :, col(p)])).astype(BF16) for bi, p in units]
    rhos = [stacked(rho_all[bi][:, col(p)]) for bi, p in units]
    afs = [stacked(at_scr[bi, :, col(p)]) for bi, p in units]
    bfs = [stacked(bt_scr[bi, :, col(p)]) for bi, p in units]
    kfs = [stacked(kt_all[bi][:, col(p)]) for bi, p in units]
    r1s = [_dot_nt(jnp.concatenate([bfs[q], kfs[q], s_scr[q].astype(BF16)], axis=0), rhos[q])
           for q in range(nu)]
    kas = [_dot_nt(kfs[q], afs[q]) for q in range(nu)]
    for q in range(nu):
        vt_scr[q] = vts[q]
        kf_scr[q] = kfs[q]
        ut_scr[q] = jnp.zeros((n, LANES), F32)
        cf_scr[q] = jnp.where(s_lt_j, r1s[q][:LANES], 0.0).astype(BF16)
        df_scr[q] = jnp.where(s_lt_j, r1s[q][LANES:2 * LANES], 0.0).astype(BF16)
        y0_scr[q] = r1s[q][2 * LANES:]
    kas = [jnp.where(s_lt_j, ka, 0.0).astype(BF16) for ka in kas]
    for q in range(nu):
        vd_scr[q] = jnp.dot(vts[q], kas[q], preferred_element_type=F32).astype(BF16)

    lane = lax.broadcasted_iota(jnp.int32, (n, LANES), 1) % n

    gsize = 8 if nu % 8 == 0 else nu
    ngroups = nu // gsize

    def steps(t8, carry):
        rows = pl.ds(pl.multiple_of(t8 * SUBLANES, SUBLANES), SUBLANES)
        for j in range(SUBLANES):
            mask = lane == t8 * SUBLANES + j
            row = lambda ref, q: ref[units[q][0], rows, col(units[q][1])][j:j + 1]
            for g in range(ngroups):
                qs = range(g * gsize, (g + 1) * gsize)
                lhs = jnp.concatenate(
                    [(s_scr[q] * row(at_scr, q)).astype(BF16)
                     + jnp.where(mask, vd_scr[q], jnp.zeros((), BF16)) for q in qs], axis=0)
                res = jnp.dot(lhs, ones, preferred_element_type=F32)
                for x, q in enumerate(qs):
                    u = res[x * n:(x + 1) * n]
                    s_scr[q] = s_scr[q] + u * row(bt_scr, q)
                    ut_scr[q] = jnp.where(mask, u, ut_scr[q])
        return carry

    lax.fori_loop(0, tb // SUBLANES, steps, 0)

    ys = []
    for q, (bi, p) in enumerate(units):
        vt = vt_scr[q]
        z_end = s_scr[q] + jnp.dot(vt, kf_scr[q], preferred_element_type=F32)
        s_scr[q] = z_end * cw_scr[bi, :, col(p)]
        yt = (y0_scr[q]
              + jnp.dot(ut_scr[q].astype(BF16), cf_scr[q], preferred_element_type=F32)
              + jnp.dot(vt, df_scr[q], preferred_element_type=F32))
        ys.append(_pair_transpose(yt)[:tb])
    stack = lambda ref: jnp.concatenate([ref[bi, :, col(p)] for bi, p in units], axis=0)
    v_all = stack(v_ref)
    y = jnp.concatenate(ys, axis=0) + v_all * stack(kr_ref)
    inv_n = 1.0 / n
    mean = _seg_sum(y, ones) * inv_n
    d = y - mean
    var = _seg_sum(d * d, ones) * inv_n
    yn = d * lax.rsqrt(var + GN_EPS)
    bonus = stack(bon_ref) * v_all
    for q, (bi, p) in enumerate(units):
        rows_q = slice(q * tb, (q + 1) * tb)
        out = (yn[rows_q] * lnw_ref[:, col(p)] + lnb_ref[:, col(p)] + bonus[rows_q]) * g_ref[bi, :, col(p)]
        y_ref[bi, :, col(p)] = out.astype(BF16)

    @pl.when(i == pl.num_programs(1) - 1)
    def _():
        for q, (bi, p) in enumerate(units):
            s = s_scr[q]
            s_out_ref[bi, 2 * p] = s[:, :n]
            s_out_ref[bi, 2 * p + 1] = s[:, n:]


def _wkv(prep, s0, nseq, tseq, dr, ln_w, ln_b):
    M = prep[0].shape[0]
    tb = min(tseq, RWKV_HEAD)
    nt = tseq // tb
    npairs = dr // LANES
    nb = 2
    assert nseq % nb == 0
    nu = nb * npairs
    prep = [a.reshape(nseq, tseq, dr) for a in prep]
    blk = pl.BlockSpec((nb, tb, dr), lambda b, i: (b, i, 0))
    st = pl.BlockSpec((nb, 2 * npairs, RWKV_HEAD, RWKV_HEAD), lambda b, i: (b, 0, 0, 0))
    vec = pl.BlockSpec((1, dr), lambda b, i: (0, 0))
    y, s_new = pl.pallas_call(
        functools.partial(_wkv_kernel, tb=tb, npairs=npairs, nb=nb),
        out_shape=[jax.ShapeDtypeStruct((nseq, tseq, dr), BF16),
                   jax.ShapeDtypeStruct(s0.shape, F32)],
        grid=(nseq // nb, nt),
        in_specs=[blk] * 9 + [st, vec, vec],
        out_specs=[blk, st],
        scratch_shapes=[pltpu.VMEM((nu, RWKV_HEAD, LANES), F32),
                        pltpu.VMEM((nu, RWKV_HEAD, LANES), BF16),
                        pltpu.VMEM((nu, RWKV_HEAD, LANES), F32),
                        pltpu.VMEM((nu, RWKV_HEAD, LANES), F32),
                        pltpu.VMEM((nu, RWKV_HEAD, LANES), BF16),
                        pltpu.VMEM((nu, LANES, LANES), BF16),
                        pltpu.VMEM((nu, LANES, LANES), BF16),
                        pltpu.VMEM((nu, LANES, LANES), BF16),
                        pltpu.VMEM((nb, tb, dr), F32), pltpu.VMEM((nb, tb, dr), F32),
                        pltpu.VMEM((nb, 1, dr), F32)],
        compiler_params=_cparams("parallel", "arbitrary"),
        name="wkv7",
    )(*prep, s0, ln_w.reshape(1, dr).astype(F32), ln_b.reshape(1, dr).astype(F32))
    return y.reshape(M, dr), s_new


def _bias_kernel(tab_ref, band_ref, dec_ref, *, nrel, qb, dec_t, dec_w):
    h = pl.program_id(0)
    m = lax.broadcasted_iota(jnp.int32, (SUBLANES, TOEP_W), 1)
    dd = jnp.where(m < TOEP_W // 2, m, m - TOEP_W)
    idx = jnp.clip(dd - ATT_CTX, -REL_CLIP, REL_CLIP) + REL_CLIP

    def pick(r, acc):
        return jnp.where(idx == r, tab_ref[h * nrel + r], acc)

    e = lax.fori_loop(0, nrel, pick, jnp.zeros((SUBLANES, TOEP_W), F32))
    eb = jnp.broadcast_to(e[0:1], (qb, TOEP_W))
    toe = pltpu.roll(eb, 0, axis=1, stride=1, stride_axis=0)
    kb = band_ref.shape[-1]
    cq = lax.broadcasted_iota(jnp.int32, (qb, kb), 0) // CHUNK
    ck = lax.broadcasted_iota(jnp.int32, (qb, kb), 1) // CHUNK
    valid = (ck >= cq) & (ck <= cq + N_PAST_CHUNKS)
    band_ref[...] = jnp.where(valid, toe[:, :kb], NEG_INF)
    dec_ref[...] = toe[:dec_t, :dec_w]


def _bias_tiles(table, qb, dec_t, dec_w):
    nrel, H = table.shape
    tab = table.T.reshape(-1).astype(F32)
    return pl.pallas_call(
        functools.partial(_bias_kernel, nrel=nrel, qb=qb, dec_t=dec_t, dec_w=dec_w),
        out_shape=[jax.ShapeDtypeStruct((H, qb, qb + ATT_CTX), F32),
                   jax.ShapeDtypeStruct((H, dec_t, dec_w), F32)],
        grid=(H,),
        in_specs=[pl.BlockSpec(memory_space=pltpu.SMEM)],
        out_specs=[pl.BlockSpec((None, qb, qb + ATT_CTX), lambda h: (h, 0, 0)),
                   pl.BlockSpec((None, dec_t, dec_w), lambda h: (h, 0, 0))],
        compiler_params=_cparams("arbitrary"),
        name="rel_bias_tiles",
    )(tab)


def _head_rms(x, g):
    return x * lax.rsqrt(jnp.mean(x * x, axis=-1, keepdims=True) + RMS_EPS) * g


def _dot_nt(a, b):
    return lax.dot_general(a, b, (((1,), (1,)), ((), ())), preferred_element_type=F32)


def _heads_per_step(nheads, col0, most):
    for hps in (4, 2, 1):
        if hps <= most and nheads % hps == 0 and col0 % (hps * ATT_HEAD) == 0:
            return hps
    raise ValueError("attention columns are not head aligned")


def _band_attn_kernel(q_ref, kc_ref, kp_ref, vc_ref, vp_ref, qg_ref, kg_ref, bias_ref,
                      o_ref, knew_ref, *, hps):
    i = pl.program_id(2)
    scale = ATT_HEAD ** -0.5
    qb = q_ref.shape[0]
    qs = 2 * CHUNK
    ks = qs + ATT_CTX
    kidx = lax.broadcasted_iota(jnp.int32, (qs, ks), 1)
    heads = [slice(hh * ATT_HEAD, (hh + 1) * ATT_HEAD) for hh in range(hps)]
    qn = [_head_rms(q_ref[:, sl], qg_ref[...]).astype(BF16) for sl in heads]
    kc = [_head_rms(kc_ref[:, sl], kg_ref[...]) for sl in heads]
    kp = [_head_rms(kp_ref[:, sl], kg_ref[...]) for sl in heads]
    k_all = [jnp.concatenate([kp[hh], kc[hh]], axis=0).astype(BF16) for hh in range(hps)]
    v_all = [jnp.concatenate([vp_ref[:, sl], vc_ref[:, sl]], axis=0).astype(BF16) for sl in heads]
    items = [(hh, r * qs) for hh in range(hps) for r in range(qb // qs)]
    s = [_dot_nt(qn[hh][r0:r0 + qs], k_all[hh][r0:r0 + ks]) * scale
         + bias_ref[hh, r0:r0 + qs, r0:r0 + ks] for hh, r0 in items]
    s = [jnp.where((i > 0) | (kidx >= ATT_CTX - r0), x, NEG_INF) for x, (hh, r0) in zip(s, items)]
    m = [jnp.max(x, axis=-1, keepdims=True) for x in s]
    p = [jnp.exp(x - y) for x, y in zip(s, m)]
    l = [jnp.sum(x, axis=-1, keepdims=True) for x in p]
    o = [jnp.dot(x.astype(BF16), v_all[hh][r0:r0 + ks], preferred_element_type=F32)
         for x, (hh, r0) in zip(p, items)]
    for x, y, (hh, r0) in zip(o, l, items):
        o_ref[r0:r0 + qs, heads[hh]] = (x / y).astype(BF16)

    @pl.when(i == pl.num_programs(2) - 1)
    def _():
        for hh in range(hps):
            knew_ref[:, heads[hh]] = kc[hh]


def _band_attention(proj, col0, nseq, tseq, nheads, q_g, k_g, bias_band):
    M = proj.shape[0]
    qb = ATT_CTX
    assert tseq % qb == 0
    nb = tseq // qb
    H = nheads
    hps = _heads_per_step(H, col0, 2)
    w = hps * ATT_HEAD
    c0 = col0 // w
    hb = H // hps
    cur = lambda off: pl.BlockSpec((qb, w), lambda b, h, i: (b * nb + i, c0 + off + h))
    prv = lambda off: pl.BlockSpec(
        (qb, w), lambda b, h, i: (b * nb + jnp.maximum(i - 1, 0), c0 + off + h))
    gspec = pl.BlockSpec((1, ATT_HEAD), lambda b, h, i: (0, 0))
    return pl.pallas_call(
        functools.partial(_band_attn_kernel, hps=hps),
        out_shape=[jax.ShapeDtypeStruct((M, H * ATT_HEAD), BF16),
                   jax.ShapeDtypeStruct((nseq, qb, H * ATT_HEAD), F32)],
        grid=(nseq, hb, nb),
        in_specs=[cur(0), cur(hb), prv(hb), cur(2 * hb), prv(2 * hb), gspec, gspec,
                  pl.BlockSpec((hps, qb, qb + ATT_CTX), lambda b, h, i: (h, 0, 0))],
        out_specs=[pl.BlockSpec((qb, w), lambda b, h, i: (b * nb + i, h)),
                   pl.BlockSpec((None, qb, w), lambda b, h, i: (b, 0, h))],
        compiler_params=_cparams("parallel", "parallel", "arbitrary"),
        name="band_attention",
    )(proj, proj, proj, proj, proj, q_g.reshape(1, -1), k_g.reshape(1, -1), bias_band)


def _cached_attn_kernel(q_ref, kn_ref, vn_ref, kc_ref, vc_ref, qg_ref, kg_ref, bias_ref,
                        o_ref, knew_ref, *, t, hps, nheads):
    scale = ATT_HEAD ** -0.5
    L = kc_ref.shape[0] // nheads
    h0 = pl.program_id(1) * hps
    heads = [slice(h * ATT_HEAD, (h + 1) * ATT_HEAD) for h in range(hps)]
    rows = [pl.ds(h0 + h, L, stride=nheads) for h in range(hps)]
    qn = [_head_rms(q_ref[:, sl], qg_ref[...]).astype(BF16) for sl in heads]
    kn = [_head_rms(kn_ref[:, sl], kg_ref[...]) for sl in heads]
    s_c = [_dot_nt(qn[h], kc_ref[rows[h], :].astype(BF16)) * scale + bias_ref[h, :, :L]
           for h in range(hps)]
    s_n = [_dot_nt(qn[h], kn[h].astype(BF16)) * scale + bias_ref[h, :, L:L + t] for h in range(hps)]
    m = [jnp.maximum(jnp.max(a, axis=-1, keepdims=True), jnp.max(b, axis=-1, keepdims=True))
         for a, b in zip(s_c, s_n)]
    p_c = [jnp.exp(a - y) for a, y in zip(s_c, m)]
    p_n = [jnp.exp(b - y) for b, y in zip(s_n, m)]
    l = [jnp.sum(a, axis=-1, keepdims=True) + jnp.sum(b, axis=-1, keepdims=True)
         for a, b in zip(p_c, p_n)]
    o = [jnp.dot(p_c[h].astype(BF16), vc_ref[rows[h], :].astype(BF16), preferred_element_type=F32)
         + jnp.dot(p_n[h].astype(BF16), vn_ref[:, heads[h]].astype(BF16), preferred_element_type=F32)
         for h in range(hps)]
    for h in range(hps):
        knew_ref[:, heads[h]] = kn[h]
        o_ref[:, heads[h]] = (o[h] / l[h]).astype(BF16)


def _cached_attention(proj, col0, nseq, tseq, nheads, k_cache, v_cache, q_g, k_g, bias_dec):
    M = proj.shape[0]
    H = nheads
    L = k_cache.shape[1] // H
    da = H * ATT_HEAD
    hps = _heads_per_step(H, col0, 4)
    w = hps * ATT_HEAD
    c0 = col0 // w
    hb = H // hps
    new = lambda off: pl.BlockSpec((tseq, w), lambda b, h: (b, c0 + off + h))
    cache = pl.BlockSpec((None, L * H, ATT_HEAD), lambda b, h: (b, 0, 0))
    gspec = pl.BlockSpec((1, ATT_HEAD), lambda b, h: (0, 0))
    out = pl.BlockSpec((tseq, w), lambda b, h: (b, h))
    return pl.pallas_call(
        functools.partial(_cached_attn_kernel, t=tseq, hps=hps, nheads=H),
        out_shape=[jax.ShapeDtypeStruct((M, da), BF16),
                   jax.ShapeDtypeStruct((M, da), F32)],
        grid=(nseq, hb),
        in_specs=[new(0), new(hb), new(2 * hb), cache, cache, gspec, gspec,
                  pl.BlockSpec((hps,) + bias_dec.shape[1:], lambda b, h: (h, 0, 0))],
        out_specs=[out, out],
        compiler_params=_cparams("parallel", "arbitrary"),
        name="cached_attention",
    )(proj, proj, proj, k_cache, v_cache, q_g.reshape(1, -1), k_g.reshape(1, -1), bias_dec)


def _outproj_kernel(yr_ref, ya_ref, w_ref, x_ref, h_ref, *, dr):
    acc = jnp.dot(yr_ref[...], w_ref[:dr, :], preferred_element_type=F32)
    acc += jnp.dot(ya_ref[...], w_ref[dr:, :], preferred_element_type=F32)
    h_ref[...] = x_ref[...] + acc


def _outproj(y_r, y_a, w_out, x):
    M, D = x.shape
    dr, da = y_r.shape[1], y_a.shape[1]
    tm = _pick_tile(M, (1024, 512, 256, 128, 64, 32))
    tn = _pick_tile(D, (512, 256, 128))
    return pl.pallas_call(
        functools.partial(_outproj_kernel, dr=dr),
        out_shape=jax.ShapeDtypeStruct((M, D), F32),
        grid=(M // tm, D // tn),
        in_specs=[
            pl.BlockSpec((tm, dr), lambda m, n: (m, 0)),
            pl.BlockSpec((tm, da), lambda m, n: (m, 0)),
            pl.BlockSpec((dr + da, tn), lambda m, n: (0, n)),
            pl.BlockSpec((tm, tn), lambda m, n: (m, n)),
        ],
        out_specs=pl.BlockSpec((tm, tn), lambda m, n: (m, n)),
        compiler_params=_cparams("parallel", "arbitrary"),
        name="out_proj",
    )(y_r, y_a, w_out, x)


def _ffn_up_kernel(h_ref, g_ref, wg_ref, wv_ref, cw_ref, cb_ref, b1_ref, b2_ref,
                   act_ref, gate_ref, hn_ref, carry_ref, *, tseq, blocks_per_seq, nsplit):
    m = pl.program_id(0)
    n = pl.program_id(1)

    @pl.when(n == 0)
    def _():
        _rms_to_bf16(h_ref, g_ref, hn_ref)

    tm = hn_ref.shape[0]
    hs = tm // nsplit
    tn = wg_ref.shape[1]
    row8 = lax.broadcasted_iota(jnp.int32, (SUBLANES, tn), 0)
    gate = None
    for c in range(nsplit):
        rows = slice(c * hs, (c + 1) * hs)
        hn = hn_ref[rows, :]
        prev_gate = gate
        gate = jnp.dot(hn, wg_ref[...], preferred_element_type=F32)
        val = jnp.dot(hn, wv_ref[...], preferred_element_type=F32)
        r1 = pltpu.roll(gate, 1, axis=0)
        r2 = pltpu.roll(gate, 2, axis=0)
        if blocks_per_seq >= 1:
            if c == 0:
                first = m % blocks_per_seq == 0
                prev = carry_ref[n]
                p2 = jnp.where(first, b2_ref[0:1, :], prev[SUBLANES - 2:SUBLANES - 1, :])
                p1 = jnp.where(first, b1_ref[0:1, :], prev[SUBLANES - 1:SUBLANES, :])
            else:
                p2 = prev_gate[hs - 2:hs - 1, :]
                p1 = prev_gate[hs - 1:hs, :]
            top1 = jnp.where(row8 == 0, p1, r1[:SUBLANES])
            top2 = jnp.where(row8 == 0, p2, jnp.where(row8 == 1, p1, r2[:SUBLANES]))
            g1 = jnp.concatenate([top1, r1[SUBLANES:]], axis=0)
            g2 = jnp.concatenate([top2, r2[SUBLANES:]], axis=0)
        else:
            t = lax.broadcasted_iota(jnp.int32, gate.shape, 0) % tseq
            g1 = jnp.where(t == 0, b1_ref[rows, :], r1)
            g2 = jnp.where(t < 2, b2_ref[rows, :], r2)
            gate_ref[rows, :] = gate
        conv = cb_ref[...] + g2 * cw_ref[0:1, :] + g1 * cw_ref[1:2, :] + gate * cw_ref[2:3, :]
        act_ref[rows, :] = (conv * _sigmoid(conv) * val).astype(BF16)
    if blocks_per_seq >= 1:
        carry_ref[n] = gate[hs - SUBLANES:, :]
        gate_ref[...] = gate[hs - (FFN_CONV - 1):, :]


def _ffn_up(h, g, w_up, conv_w, conv_b, conv_prev, nseq, tseq):
    M, D = h.shape
    Fd = w_up.shape[1] // 2
    tn = _pick_tile(Fd, (256, 128))
    nn = Fd // tn
    nsplit = 4
    if tseq >= 512:
        tm = _pick_tile(tseq, (1024, 512))
        bps = tseq // tm
        b1 = conv_prev[:, 1:2, :]
        b2 = conv_prev[:, 0:1, :]
        bspec = pl.BlockSpec((None, 1, tn), lambda m, n: (m // bps, 0, n))
        gate_shape = jax.ShapeDtypeStruct((M // tm, FFN_CONV - 1, Fd), F32)
        gate_spec = pl.BlockSpec((None, FFN_CONV - 1, tn), lambda m, n: (m, 0, n))
    else:
        tm = M
        bps = 0
        z = jnp.zeros((nseq, tseq, Fd), F32)
        b1 = z.at[:, 0].set(conv_prev[:, 1]).reshape(M, Fd)
        b2 = z.at[:, 0].set(conv_prev[:, 0]).at[:, 1].set(conv_prev[:, 1]).reshape(M, Fd)
        bspec = pl.BlockSpec((tm, tn), lambda m, n: (m, n))
        gate_shape = jax.ShapeDtypeStruct((M, Fd), F32)
        gate_spec = pl.BlockSpec((tm, tn), lambda m, n: (m, n))
    act, gate = pl.pallas_call(
        functools.partial(_ffn_up_kernel, tseq=tseq, blocks_per_seq=bps, nsplit=nsplit),
        out_shape=[jax.ShapeDtypeStruct((M, Fd), BF16), gate_shape],
        grid=(M // tm, nn),
        in_specs=[
            pl.BlockSpec((tm, D), lambda m, n: (m, 0), pipeline_mode=pl.Buffered(1)),
            pl.BlockSpec((1, D), lambda m, n: (0, 0)),
            pl.BlockSpec((D, tn), lambda m, n: (0, n)),
            pl.BlockSpec((D, tn), lambda m, n: (0, nn + n)),
            pl.BlockSpec((FFN_CONV, tn), lambda m, n: (0, n)),
            pl.BlockSpec((1, tn), lambda m, n: (0, n)),
            bspec, bspec,
        ],
        out_specs=[pl.BlockSpec((tm, tn), lambda m, n: (m, n)), gate_spec],
        scratch_shapes=[pltpu.VMEM((tm, D), BF16), pltpu.VMEM((nn, SUBLANES, tn), F32)],
        compiler_params=_cparams("arbitrary", "arbitrary"),
        name="ffn_up",
    )(h, g.reshape(1, D), w_up, w_up, conv_w, conv_b.reshape(1, Fd), b1, b2)
    if bps == 0:
        gate = gate.reshape(nseq, tseq, Fd)[:, tseq - (FFN_CONV - 1):]
    else:
        gate = gate[bps - 1::bps]
    return act, gate


def _ffn_down_kernel(a_ref, w_ref, h_ref, o_ref):
    o_ref[...] = h_ref[...] + jnp.dot(a_ref[...], w_ref[...], preferred_element_type=F32)


def _ffn_down(act, w_down, h):
    M, Fd = act.shape
    D = w_down.shape[1]
    tm = _pick_tile(M, (512, 256, 128, 64, 32))
    tn = _pick_tile(D, (256, 128))
    return pl.pallas_call(
        _ffn_down_kernel,
        out_shape=jax.ShapeDtypeStruct((M, D), F32),
        grid=(M // tm, D // tn),
        in_specs=[
            pl.BlockSpec((tm, Fd), lambda m, n: (m, 0)),
            pl.BlockSpec((Fd, tn), lambda m, n: (0, n)),
            pl.BlockSpec((tm, tn), lambda m, n: (m, n)),
        ],
        out_specs=pl.BlockSpec((tm, tn), lambda m, n: (m, n)),
        compiler_params=_cparams("parallel", "arbitrary"),
        name="ffn_down",
    )(act, w_down, h)


def _layer_weights(lp):
    (attn_norm_g, w_in, mu, w0, w2, a0, a2, g2, k_k, k_a, r_k, ln_w, ln_b, q_g, k_g, rel_bias,
     w_out, ffn_norm_g, w_up, conv_w, conv_b, w_down) = lp
    D = w_in.shape[0]
    dr = w0.shape[0]
    dl, al, gl = w2.shape[0], a2.shape[0], g2.shape[0]
    nrc = 3 * dr + dl + al + gl
    lpad = _round_up(dl + al + gl, LANES)
    cr = 3 * dr + lpad
    assert cr <= w_in.shape[1]
    w_in_r = w_in[:, :cr].astype(BF16)
    w_in_a = w_in[:, nrc:].astype(BF16)
    mu_p = jnp.concatenate([mu, jnp.zeros((cr - nrc,), mu.dtype)]).reshape(1, cr)
    zrow = lambda n: jnp.zeros((n, dr), F32)
    w2p = jnp.concatenate([w2, zrow(lpad - dl)], axis=0).astype(BF16)
    a2p = jnp.concatenate([zrow(dl), a2, zrow(lpad - dl - al)], axis=0).astype(BF16)
    g2p = jnp.concatenate([zrow(dl + al), g2, zrow(lpad - dl - al - gl)], axis=0).astype(BF16)
    return dict(attn_norm_g=attn_norm_g, w_in_r=w_in_r, w_in_a=w_in_a, mu_p=mu_p, w0=w0, a0=a0, k_k=k_k, k_a=k_a,
                r_k=r_k.reshape(-1), w2p=w2p, a2p=a2p, g2p=g2p, ln_w=ln_w, ln_b=ln_b, q_g=q_g, k_g=k_g,
                rel_bias=rel_bias, w_out=w_out.astype(BF16), ffn_norm_g=ffn_norm_g,
                w_up=w_up.astype(BF16), conv_w=conv_w, conv_b=conv_b, w_down=w_down.astype(BF16),
                nrc=nrc, cr=cr, dr=dr)


def _block(x, shift_prev, wkv_prev, conv_prev, k_cache, v_cache, W, bias_band, bias_dec):
    B, T, D = x.shape
    M = B * T
    dr, cr, nrc = W["dr"], W["cr"], W["nrc"]
    H = (D - dr) // ATT_HEAD
    xf = x.reshape(M, D)
    proj = _rms_matmul(xf, W["attn_norm_g"], W["w_in_r"], W["w_in_a"])

    shift_p = jnp.concatenate([shift_prev, jnp.zeros((B, 1, cr - nrc), F32)], axis=-1)
    prep = _rwkv_prep(proj, shift_p, B, T, cr, dr, W["mu_p"], W["w0"], W["a0"], W["k_k"], W["k_a"],
                      W["r_k"], W["w2p"], W["a2p"], W["g2p"])
    y_r, wkv_new = _wkv(prep, wkv_prev, B, T, dr, W["ln_w"], W["ln_b"])
    shift_new = proj.reshape(B, T, -1)[:, T - 1:, :nrc]

    da = H * ATT_HEAD
    if k_cache is None:
        y_a, k_new = _band_attention(proj, cr, B, T, H, W["q_g"], W["k_g"], bias_band)
        keep = min(ATT_CTX, T)
        k_new = k_new.reshape(B, keep, H, ATT_HEAD)
        v_new = proj.reshape(B, T, -1)[:, T - keep:, cr + 2 * da:].reshape(B, keep, H, ATT_HEAD)
    else:
        L = k_cache.shape[1]
        y_a, k_new = _cached_attention(proj, cr, B, T, H, k_cache.reshape(B, L * H, ATT_HEAD),
                                       v_cache.reshape(B, L * H, ATT_HEAD), W["q_g"], W["k_g"], bias_dec)
        k_new = k_new.reshape(B, T, H, ATT_HEAD)
        v_new = proj.reshape(B, T, -1)[:, :, cr + 2 * da:].reshape(B, T, H, ATT_HEAD)

    h = _outproj(y_r, y_a, W["w_out"], xf)
    act, conv_new = _ffn_up(h, W["ffn_norm_g"], W["w_up"], W["conv_w"], W["conv_b"], conv_prev, B, T)
    y = _ffn_down(act, W["w_down"], h)
    return y.reshape(B, T, D), shift_new, wkv_new, k_new, v_new, conv_new


def kernel(x_prompt, x_sample, state_rwkv_shift, state_rwkv_wkv, cache_attn_k, cache_attn_v,
           state_ffn_conv, attn_norm_g, w_in, rwkv_mu, rwkv_w0, rwkv_w2, rwkv_a0, rwkv_a2, rwkv_g2,
           rwkv_k_k, rwkv_k_a, rwkv_r_k, rwkv_ln_w, rwkv_ln_b, att_q_norm_g, att_k_norm_g,
           att_rel_bias, w_out, ffn_norm_g, ffn_w_up, ffn_conv_w, ffn_conv_b, ffn_w_down):
    depth = w_in.shape[0]
    B = x_prompt.shape[0]
    xp, xs = x_prompt, x_sample
    dec_t = x_sample.shape[1]
    L = cache_attn_k.shape[2]
    assert L == ATT_CTX
    dec_w = _round_up(L + dec_t, LANES)
    p_states = ([], [], [], [], [])
    s_states = ([], [], [], [], [])
    for l in range(depth):
        lp = (attn_norm_g[l], w_in[l], rwkv_mu[l], rwkv_w0[l], rwkv_w2[l], rwkv_a0[l], rwkv_a2[l],
              rwkv_g2[l], rwkv_k_k[l], rwkv_k_a[l], rwkv_r_k[l], rwkv_ln_w[l], rwkv_ln_b[l],
              att_q_norm_g[l], att_k_norm_g[l], att_rel_bias[l], w_out[l], ffn_norm_g[l],
              ffn_w_up[l], ffn_conv_w[l], ffn_conv_b[l], ffn_w_down[l])
        W = _layer_weights(lp)
        nrc, dr = W["nrc"], W["dr"]
        Fd = ffn_conv_b.shape[-1]
        bias_band, bias_dec = _bias_tiles(att_rel_bias[l], ATT_CTX, dec_t, dec_w)
        shift0 = jnp.zeros((B, 1, nrc), F32)
        wkv0 = jnp.zeros((B, dr // RWKV_HEAD, RWKV_HEAD, RWKV_HEAD), F32)
        conv0 = jnp.zeros((B, FFN_CONV - 1, Fd), F32)
        xp, *ps = _block(xp, shift0, wkv0, conv0, None, None, W, bias_band, bias_dec)
        xs, *ss = _block(xs, state_rwkv_shift[l], state_rwkv_wkv[l], state_ffn_conv[l],
                         cache_attn_k[l], cache_attn_v[l], W, bias_band, bias_dec)
        for i in range(5):
            p_states[i].append(ps[i])
            s_states[i].append(ss[i])
    p_out = [jnp.stack(t, axis=0) for t in p_states]
    s_out = [jnp.stack(t, axis=0) for t in s_states]
    return (xp, xs, *p_out, *s_out)
```

```python
import functools
import math

import jax
import jax.numpy as jnp
from jax import lax
from jax.experimental import pallas as pl
from jax.experimental.pallas import tpu as pltpu

F32 = jnp.float32
BF16 = jnp.bfloat16

RMS_EPS = 1e-6
GN_EPS = 64e-5
NEG_INF = -1e30
CHUNK = 64
N_PAST_CHUNKS = 8
ATT_CTX = N_PAST_CHUNKS * CHUNK
REL_CLIP = 128
RWKV_HEAD = 64
ATT_HEAD = 128
FFN_CONV = 3

LANES = 128
SUBLANES = 8
VMEM_LIMIT = 56 * 1024 * 1024
TOEP_W = 2048


def _cparams(*sem):
    return pltpu.CompilerParams(dimension_semantics=sem, vmem_limit_bytes=VMEM_LIMIT)


def _round_up(n, m):
    return (n + m - 1) // m * m


def _pick_tile(n, candidates):
    for c in candidates:
        if n % c == 0:
            return c
    raise ValueError(f"no tile in {candidates} divides {n}")


def _pair_ones():
    r = lax.broadcasted_iota(jnp.int32, (LANES, LANES), 0) // RWKV_HEAD
    c = lax.broadcasted_iota(jnp.int32, (LANES, LANES), 1) // RWKV_HEAD
    return (r == c).astype(BF16)


def _seg_sum(x, ones):
    hi = x.astype(BF16)
    lo = (x - hi.astype(F32)).astype(BF16)
    return (jnp.dot(hi, ones, preferred_element_type=F32)
            + jnp.dot(lo, ones, preferred_element_type=F32))


def _seg_sum_wide(x, ones):
    rows, width = x.shape
    blocks = [slice(c * LANES, (c + 1) * LANES) for c in range(width // LANES)]
    r = _seg_sum(jnp.concatenate([x[:, sl] for sl in blocks], axis=0), ones)
    return jnp.concatenate([r[c * rows:(c + 1) * rows] for c in range(len(blocks))], axis=1)


def _sigmoid(z):
    return 1.0 / (1.0 + jnp.exp(-z))


def _rms_to_bf16(x_ref, g_ref, xn_ref):
    tm = x_ref.shape[0]
    rc = min(tm, 256)
    for c in range(tm // rc):
        rows = slice(c * rc, (c + 1) * rc)
        x = x_ref[rows, :]
        ms = jnp.mean(x * x, axis=-1, keepdims=True)
        xn_ref[rows, :] = (x * lax.rsqrt(ms + RMS_EPS) * g_ref[...]).astype(BF16)


def _rms_matmul_kernel(x_ref, g_ref, wa_ref, wb_ref, o_ref, xn_ref, *, na):
    n = pl.program_id(1)

    @pl.when(n == 0)
    def _():
        _rms_to_bf16(x_ref, g_ref, xn_ref)

    @pl.when(n < na)
    def _():
        o_ref[...] = jnp.dot(xn_ref[...], wa_ref[...], preferred_element_type=F32)

    @pl.when(n >= na)
    def _():
        o_ref[...] = jnp.dot(xn_ref[...], wb_ref[...], preferred_element_type=F32)


def _rms_matmul(x, g, wa, wb):
    M, D = x.shape
    Na, Nb = wa.shape[1], wb.shape[1]
    tm = _pick_tile(M, (1024, 512, 256, 128, 64, 32))
    tn = _pick_tile(math.gcd(Na, Nb), (512, 256, 128))
    na, nb = Na // tn, Nb // tn
    return pl.pallas_call(
        functools.partial(_rms_matmul_kernel, na=na),
        out_shape=jax.ShapeDtypeStruct((M, Na + Nb), F32),
        grid=(M // tm, na + nb),
        in_specs=[
            pl.BlockSpec((tm, D), lambda m, n: (m, 0), pipeline_mode=pl.Buffered(1)),
            pl.BlockSpec((1, D), lambda m, n: (0, 0)),
            pl.BlockSpec((D, tn), lambda m, n: (0, jnp.minimum(n, na - 1))),
            pl.BlockSpec((D, tn), lambda m, n: (0, jnp.maximum(n - na, 0))),
        ],
        out_specs=pl.BlockSpec((tm, tn), lambda m, n: (m, n)),
        scratch_shapes=[pltpu.VMEM((tm, D), BF16)],
        compiler_params=_cparams("parallel", "arbitrary"),
        name="rms_inproj",
    )(x, g.reshape(1, D), wa, wb)


def _rwkv_prep_kernel(p_ref, prev_ref, mu_ref, w0_ref, a0_ref, kk_ref, ka_ref, rk_ref,
                      w2_ref, a2_ref, g2_ref,
                      a_out, b_out, w_out, k_out, rp_out, v_out, g_out, kr_out, bon_out,
                      last_ref, *, dr):
    @pl.when(pl.program_id(1) == 0)
    def _():
        last_ref[...] = prev_ref[...]

    p = p_ref[...]
    tb = p.shape[0]
    row = lax.broadcasted_iota(jnp.int32, p.shape, 0)
    shifted = jnp.where(row == 0, last_ref[...], pltpu.roll(p, 1, axis=0))
    last_ref[...] = p[tb - 1:tb, :]
    xm = p + (shifted - p) * mu_ref[...]

    lo = xm[:, 3 * dr:]
    wl = jnp.dot(jnp.tanh(lo).astype(BF16), w2_ref[...], preferred_element_type=F32)
    al = jnp.dot(lo.astype(BF16), a2_ref[...], preferred_element_type=F32)
    g_out[...] = jnp.dot(_sigmoid(lo).astype(BF16), g2_ref[...], preferred_element_type=F32)

    ones = _pair_ones()
    r = xm[:, :dr]
    k = xm[:, dr:2 * dr]
    v_out[...] = xm[:, 2 * dr:3 * dr]
    decay = jnp.exp(-math.exp(-0.5) * _sigmoid(w0_ref[...] + wl))
    asig = _sigmoid(a0_ref[...] + al)
    kk = k * kk_ref[...]
    kk = kk * lax.rsqrt(jnp.maximum(_seg_sum_wide(kk * kk, ones), 1e-24))
    k2 = k * (1.0 + (asig - 1.0) * ka_ref[...])
    b_s = kk * asig
    rk2 = r * k2
    a_out[...] = -kk
    b_out[...] = b_s
    w_out[...] = decay
    k_out[...] = k2
    rp_out[...] = decay * r - kk * _seg_sum_wide(b_s * r, ones)
    kr_out[...] = _seg_sum_wide(rk2, ones)
    bon_out[...] = _seg_sum_wide(rk2 * rk_ref[...], ones)


def _rwkv_prep(proj, shift_prev, nseq, tseq, cr, dr, mu_p, w0, a0, k_k, k_a, r_k, w2p, a2p, g2p):
    M = proj.shape[0]
    tb = _pick_tile(tseq, (128, 64, 32))
    nt = tseq // tb
    lp = cr - 3 * dr
    row = lambda a: a.reshape(1, -1).astype(F32)
    vec_spec = pl.BlockSpec((1, dr), lambda b, i: (0, 0))
    mat_spec = pl.BlockSpec((lp, dr), lambda b, i: (0, 0))
    out_spec = pl.BlockSpec((tb, dr), lambda b, i: (b * nt + i, 0))
    outs = pl.pallas_call(
        functools.partial(_rwkv_prep_kernel, dr=dr),
        out_shape=[jax.ShapeDtypeStruct((M, dr), F32)] * 9,
        grid=(nseq, nt),
        in_specs=[
            pl.BlockSpec((tb, cr), lambda b, i: (b * nt + i, 0)),
            pl.BlockSpec((None, 1, cr), lambda b, i: (b, 0, 0)),
            pl.BlockSpec((1, cr), lambda b, i: (0, 0)),
            vec_spec, vec_spec, vec_spec, vec_spec, vec_spec,
            mat_spec, mat_spec, mat_spec,
        ],
        out_specs=[out_spec] * 9,
        scratch_shapes=[pltpu.VMEM((1, cr), F32)],
        compiler_params=_cparams("parallel", "arbitrary"),
        name="rwkv_prep",
    )(proj, shift_prev, mu_p, row(w0), row(a0), row(k_k), row(k_a), row(r_k), w2p, a2p, g2p)
    return outs


def _pair_transpose(x):
    t2 = jnp.concatenate([x, x], axis=0).T
    lane = lax.broadcasted_iota(jnp.int32, x.shape, 1)
    return jnp.where(lane < RWKV_HEAD, t2[:RWKV_HEAD], t2[RWKV_HEAD:])


def _stack_heads(x):
    lane = lax.broadcasted_iota(jnp.int32, x.shape, 1)
    return jnp.concatenate([jnp.where(lane < RWKV_HEAD, x, 0.0),
                            jnp.where(lane >= RWKV_HEAD, x, 0.0)], axis=0)


def _wkv_kernel(a_ref, b_ref, w_ref, k_ref, rp_ref, v_ref, g_ref, kr_ref, bon_ref,
                s0_ref, lnw_ref, lnb_ref, y_ref, s_out_ref,
                s_scr, vt_scr, ut_scr, y0_scr, vd_scr, cf_scr, df_scr, kf_scr,
                at_scr, bt_scr, cw_scr, *, tb, npairs, nb):
    i = pl.program_id(1)
    n = RWKV_HEAD

    units = [(bi, p) for bi in range(nb) for p in range(npairs)]
    nu = len(units)
    col = lambda p: slice(p * LANES, (p + 1) * LANES)

    @pl.when(i == 0)
    def _():
        for q, (bi, p) in enumerate(units):
            s_scr[q] = jnp.concatenate([s0_ref[bi, 2 * p], s0_ref[bi, 2 * p + 1]], axis=1)

    ones = _pair_ones()
    s_lt_j = (lax.broadcasted_iota(jnp.int32, (LANES, LANES), 0) % n
              < lax.broadcasted_iota(jnp.int32, (LANES, LANES), 1) % n)

    def pad_rows(x):
        return x if tb == n else jnp.concatenate([x, jnp.zeros((n - tb, LANES), F32)], axis=0)

    trow_all = lax.broadcasted_iota(jnp.int32, w_ref.shape[1:], 0)
    kt_all, rho_all = [], []
    for bi in range(nb):
        cw = w_ref[bi]
        shift = 1
        while shift < tb:
            cw = jnp.where(trow_all >= shift, cw * pltpu.roll(cw, shift, axis=0), cw)
            shift *= 2
        cw_prev = jnp.where(trow_all == 0, 1.0, pltpu.roll(cw, 1, axis=0))
        inv = 1.0 / cw
        at_scr[bi] = a_ref[bi] * cw_prev
        bt_scr[bi] = b_ref[bi] * inv
        cw_scr[bi] = cw[tb - 1:tb]
        kt_all.append(k_ref[bi] * inv)
        rho_all.append(rp_ref[bi] * cw_prev)

    stacked = lambda x: _stack_heads(pad_rows(x)).astype(BF16)
    vts = [_pair_transpose(pad_rows(v_ref[bi, :, col(p)])).astype(BF16) for bi, p in units]
    rhos = [stacked(rho_all[bi][:, col(p)]) for bi, p in units]
    afs = [stacked(at_scr[bi, :, col(p)]) for bi, p in units]
    bfs = [stacked(bt_scr[bi, :, col(p)]) for bi, p in units]
    kfs = [stacked(kt_all[bi][:, col(p)]) for bi, p in units]
    r1s = [_dot_nt(jnp.concatenate([bfs[q], kfs[q], s_scr[q].astype(BF16)], axis=0), rhos[q])
           for q in range(nu)]
    kas = [_dot_nt(kfs[q], afs[q]) for q in range(nu)]
    for q in range(nu):
        vt_scr[q] = vts[q]
        kf_scr[q] = kfs[q]
        ut_scr[q] = jnp.zeros((n, LANES), F32)
        cf_scr[q] = jnp.where(s_lt_j, r1s[q][:LANES], 0.0).astype(BF16)
        df_scr[q] = jnp.where(s_lt_j, r1s[q][LANES:2 * LANES], 0.0).astype(BF16)
        y0_scr[q] = r1s[q][2 * LANES:]
    kas = [jnp.where(s_lt_j, ka, 0.0).astype(BF16) for ka in kas]
    for q in range(nu):
        vd_scr[q] = jnp.dot(vts[q], kas[q], preferred_element_type=F32).astype(BF16)

    lane = lax.broadcasted_iota(jnp.int32, (n, LANES), 1) % n

    gsize = 8 if nu % 8 == 0 else nu
    ngroups = nu // gsize

    def steps(t8, carry):
        rows = pl.ds(pl.multiple_of(t8 * SUBLANES, SUBLANES), SUBLANES)
        for j in range(SUBLANES):
            mask = lane == t8 * SUBLANES + j
            row = lambda ref, q: ref[units[q][0], rows, col(units[q][1])][j:j + 1]
            for g in range(ngroups):
                qs = range(g * gsize, (g + 1) * gsize)
                lhs = jnp.concatenate(
                    [(s_scr[q] * row(at_scr, q)).astype(BF16)
                     + jnp.where(mask, vd_scr[q], jnp.zeros((), BF16)) for q in qs], axis=0)
                res = jnp.dot(lhs, ones, preferred_element_type=F32)
                for x, q in enumerate(qs):
                    u = res[x * n:(x + 1) * n]
                    s_scr[q] = s_scr[q] + u * row(bt_scr, q)
                    pltpu.store(ut_scr.at[q], u, mask=mask)
        return carry

    lax.fori_loop(0, tb // SUBLANES, steps, 0)

    ys = []
    for q, (bi, p) in enumerate(units):
        vt = vt_scr[q]
        z_end = s_scr[q] + jnp.dot(vt, kf_scr[q], preferred_element_type=F32)
        s_scr[q] = z_end * cw_scr[bi, :, col(p)]
        yt = (y0_scr[q]
              + jnp.dot(ut_scr[q].astype(BF16), cf_scr[q], preferred_element_type=F32)
              + jnp.dot(vt, df_scr[q], preferred_element_type=F32))
        ys.append(_pair_transpose(yt)[:tb])
    stack = lambda ref: jnp.concatenate([ref[bi, :, col(p)] for bi, p in units], axis=0)
    v_all = stack(v_ref)
    y = jnp.concatenate(ys, axis=0) + v_all * stack(kr_ref)
    inv_n = 1.0 / n
    mean = _seg_sum(y, ones) * inv_n
    d = y - mean
    var = _seg_sum(d * d, ones) * inv_n
    yn = d * lax.rsqrt(var + GN_EPS)
    bonus = stack(bon_ref) * v_all
    for q, (bi, p) in enumerate(units):
        rows_q = slice(q * tb, (q + 1) * tb)
        out = (yn[rows_q] * lnw_ref[:, col(p)] + lnb_ref[:, col(p)] + bonus[rows_q]) * g_ref[bi, :, col(p)]
        y_ref[bi, :, col(p)] = out.astype(BF16)

    @pl.when(i == pl.num_programs(1) - 1)
    def _():
        for q, (bi, p) in enumerate(units):
            s = s_scr[q]
            s_out_ref[bi, 2 * p] = s[:, :n]
            s_out_ref[bi, 2 * p + 1] = s[:, n:]


def _wkv(prep, s0, nseq, tseq, dr, ln_w, ln_b):
    M = prep[0].shape[0]
    tb = min(tseq, RWKV_HEAD)
    nt = tseq // tb
    npairs = dr // LANES
    nb = 2
    assert nseq % nb == 0
    nu = nb * npairs
    prep = [a.reshape(nseq, tseq, dr) for a in prep]
    blk = pl.BlockSpec((nb, tb, dr), lambda b, i: (b, i, 0))
    st = pl.BlockSpec((nb, 2 * npairs, RWKV_HEAD, RWKV_HEAD), lambda b, i: (b, 0, 0, 0))
    vec = pl.BlockSpec((1, dr), lambda b, i: (0, 0))
    y, s_new = pl.pallas_call(
        functools.partial(_wkv_kernel, tb=tb, npairs=npairs, nb=nb),
        out_shape=[jax.ShapeDtypeStruct((nseq, tseq, dr), BF16),
                   jax.ShapeDtypeStruct(s0.shape, F32)],
        grid=(nseq // nb, nt),
        in_specs=[blk] * 9 + [st, vec, vec],
        out_specs=[blk, st],
        scratch_shapes=[pltpu.VMEM((nu, RWKV_HEAD, LANES), F32),
                        pltpu.VMEM((nu, RWKV_HEAD, LANES), BF16),
                        pltpu.VMEM((nu, RWKV_HEAD, LANES), F32),
                        pltpu.VMEM((nu, RWKV_HEAD, LANES), F32),
                        pltpu.VMEM((nu, RWKV_HEAD, LANES), BF16),
                        pltpu.VMEM((nu, LANES, LANES), BF16),
                        pltpu.VMEM((nu, LANES, LANES), BF16),
                        pltpu.VMEM((nu, LANES, LANES), BF16),
                        pltpu.VMEM((nb, tb, dr), F32), pltpu.VMEM((nb, tb, dr), F32),
                        pltpu.VMEM((nb, 1, dr), F32)],
        compiler_params=_cparams("parallel", "arbitrary"),
        name="wkv7",
    )(*prep, s0, ln_w.reshape(1, dr).astype(F32), ln_b.reshape(1, dr).astype(F32))
    return y.reshape(M, dr), s_new


def _bias_kernel(tab_ref, band_ref, dec_ref, *, nrel, qb, dec_t, dec_w):
    h = pl.program_id(0)
    m = lax.broadcasted_iota(jnp.int32, (SUBLANES, TOEP_W), 1)
    dd = jnp.where(m < TOEP_W // 2, m, m - TOEP_W)
    idx = jnp.clip(dd - ATT_CTX, -REL_CLIP, REL_CLIP) + REL_CLIP

    def pick(r, acc):
        return jnp.where(idx == r, tab_ref[h * nrel + r], acc)

    e = lax.fori_loop(0, nrel, pick, jnp.zeros((SUBLANES, TOEP_W), F32))
    eb = jnp.broadcast_to(e[0:1], (qb, TOEP_W))
    toe = pltpu.roll(eb, 0, axis=1, stride=1, stride_axis=0)
    kb = band_ref.shape[-1]
    cq = lax.broadcasted_iota(jnp.int32, (qb, kb), 0) // CHUNK
    ck = lax.broadcasted_iota(jnp.int32, (qb, kb), 1) // CHUNK
    valid = (ck >= cq) & (ck <= cq + N_PAST_CHUNKS)
    band_ref[...] = jnp.where(valid, toe[:, :kb], NEG_INF)
    dec_ref[...] = toe[:dec_t, :dec_w]


def _bias_tiles(table, qb, dec_t, dec_w):
    nrel, H = table.shape
    tab = table.T.reshape(-1).astype(F32)
    return pl.pallas_call(
        functools.partial(_bias_kernel, nrel=nrel, qb=qb, dec_t=dec_t, dec_w=dec_w),
        out_shape=[jax.ShapeDtypeStruct((H, qb, qb + ATT_CTX), F32),
                   jax.ShapeDtypeStruct((H, dec_t, dec_w), F32)],
        grid=(H,),
        in_specs=[pl.BlockSpec(memory_space=pltpu.SMEM)],
        out_specs=[pl.BlockSpec((None, qb, qb + ATT_CTX), lambda h: (h, 0, 0)),
                   pl.BlockSpec((None, dec_t, dec_w), lambda h: (h, 0, 0))],
        compiler_params=_cparams("arbitrary"),
        name="rel_bias_tiles",
    )(tab)


def _head_rms(x, g):
    return x * lax.rsqrt(jnp.mean(x * x, axis=-1, keepdims=True) + RMS_EPS) * g


def _dot_nt(a, b):
    return lax.dot_general(a, b, (((1,), (1,)), ((), ())), preferred_element_type=F32)


def _heads_per_step(nheads, col0, most):
    for hps in (4, 2, 1):
        if hps <= most and nheads % hps == 0 and col0 % (hps * ATT_HEAD) == 0:
            return hps
    raise ValueError("attention columns are not head aligned")


def _band_attn_kernel(q_ref, kc_ref, kp_ref, vc_ref, vp_ref, qg_ref, kg_ref, bias_ref,
                      o_ref, knew_ref, *, hps):
    i = pl.program_id(2)
    scale = ATT_HEAD ** -0.5
    qb = q_ref.shape[0]
    qs = 2 * CHUNK
    ks = qs + ATT_CTX
    kidx = lax.broadcasted_iota(jnp.int32, (qs, ks), 1)
    heads = [slice(hh * ATT_HEAD, (hh + 1) * ATT_HEAD) for hh in range(hps)]
    qn = [_head_rms(q_ref[:, sl], qg_ref[...]).astype(BF16) for sl in heads]
    kc = [_head_rms(kc_ref[:, sl], kg_ref[...]) for sl in heads]
    kp = [_head_rms(kp_ref[:, sl], kg_ref[...]) for sl in heads]
    k_all = [jnp.concatenate([kp[hh], kc[hh]], axis=0).astype(BF16) for hh in range(hps)]
    v_all = [jnp.concatenate([vp_ref[:, sl], vc_ref[:, sl]], axis=0).astype(BF16) for sl in heads]
    items = [(hh, r * qs) for hh in range(hps) for r in range(qb // qs)]
    s = [_dot_nt(qn[hh][r0:r0 + qs], k_all[hh][r0:r0 + ks]) * scale
         + bias_ref[hh, r0:r0 + qs, r0:r0 + ks] for hh, r0 in items]
    s = [jnp.where((i > 0) | (kidx >= ATT_CTX - r0), x, NEG_INF) for x, (hh, r0) in zip(s, items)]
    m = [jnp.max(x, axis=-1, keepdims=True) for x in s]
    p = [jnp.exp(x - y) for x, y in zip(s, m)]
    l = [jnp.sum(x, axis=-1, keepdims=True) for x in p]
    o = [jnp.dot(x.astype(BF16), v_all[hh][r0:r0 + ks], preferred_element_type=F32)
         for x, (hh, r0) in zip(p, items)]
    for x, y, (hh, r0) in zip(o, l, items):
        o_ref[r0:r0 + qs, heads[hh]] = (x / y).astype(BF16)

    @pl.when(i == pl.num_programs(2) - 1)
    def _():
        for hh in range(hps):
            knew_ref[:, heads[hh]] = kc[hh]


def _band_attention(proj, col0, nseq, tseq, nheads, q_g, k_g, bias_band):
    M = proj.shape[0]
    qb = ATT_CTX
    assert tseq % qb == 0
    nb = tseq // qb
    H = nheads
    hps = _heads_per_step(H, col0, 2)
    w = hps * ATT_HEAD
    c0 = col0 // w
    hb = H // hps
    cur = lambda off: pl.BlockSpec((qb, w), lambda b, h, i: (b * nb + i, c0 + off + h))
    prv = lambda off: pl.BlockSpec(
        (qb, w), lambda b, h, i: (b * nb + jnp.maximum(i - 1, 0), c0 + off + h))
    gspec = pl.BlockSpec((1, ATT_HEAD), lambda b, h, i: (0, 0))
    return pl.pallas_call(
        functools.partial(_band_attn_kernel, hps=hps),
        out_shape=[jax.ShapeDtypeStruct((M, H * ATT_HEAD), BF16),
                   jax.ShapeDtypeStruct((nseq, qb, H * ATT_HEAD), F32)],
        grid=(nseq, hb, nb),
        in_specs=[cur(0), cur(hb), prv(hb), cur(2 * hb), prv(2 * hb), gspec, gspec,
                  pl.BlockSpec((hps, qb, qb + ATT_CTX), lambda b, h, i: (h, 0, 0))],
        out_specs=[pl.BlockSpec((qb, w), lambda b, h, i: (b * nb + i, h)),
                   pl.BlockSpec((None, qb, w), lambda b, h, i: (b, 0, h))],
        compiler_params=_cparams("parallel", "parallel", "arbitrary"),
        name="band_attention",
    )(proj, proj, proj, proj, proj, q_g.reshape(1, -1), k_g.reshape(1, -1), bias_band)


def _cached_attn_kernel(q_ref, kn_ref, vn_ref, kc_ref, vc_ref, qg_ref, kg_ref, bias_ref,
                        o_ref, knew_ref, *, t, hps, nheads):
    scale = ATT_HEAD ** -0.5
    L = kc_ref.shape[0] // nheads
    h0 = pl.program_id(1) * hps
    heads = [slice(h * ATT_HEAD, (h + 1) * ATT_HEAD) for h in range(hps)]
    rows = [pl.ds(h0 + h, L, stride=nheads) for h in range(hps)]
    qn = [_head_rms(q_ref[:, sl], qg_ref[...]).astype(BF16) for sl in heads]
    kn = [_head_rms(kn_ref[:, sl], kg_ref[...]) for sl in heads]
    s_c = [_dot_nt(qn[h], kc_ref[rows[h], :].astype(BF16)) * scale + bias_ref[h, :, :L]
           for h in range(hps)]
    s_n = [_dot_nt(qn[h], kn[h].astype(BF16)) * scale + bias_ref[h, :, L:L + t] for h in range(hps)]
    m = [jnp.maximum(jnp.max(a, axis=-1, keepdims=True), jnp.max(b, axis=-1, keepdims=True))
         for a, b in zip(s_c, s_n)]
    p_c = [jnp.exp(a - y) for a, y in zip(s_c, m)]
    p_n = [jnp.exp(b - y) for b, y in zip(s_n, m)]
    l = [jnp.sum(a, axis=-1, keepdims=True) + jnp.sum(b, axis=-1, keepdims=True)
         for a, b in zip(p_c, p_n)]
    o = [jnp.dot(p_c[h].astype(BF16), vc_ref[rows[h], :].astype(BF16), preferred_element_type=F32)
         + jnp.dot(p_n[h].astype(BF16), vn_ref[:, heads[h]].astype(BF16), preferred_element_type=F32)
         for h in range(hps)]
    for h in range(hps):
        knew_ref[:, heads[h]] = kn[h]
        o_ref[:, heads[h]] = (o[h] / l[h]).astype(BF16)


def _cached_attention(proj, col0, nseq, tseq, nheads, k_cache, v_cache, q_g, k_g, bias_dec):
    M = proj.shape[0]
    H = nheads
    L = k_cache.shape[1] // H
    da = H * ATT_HEAD
    hps = _heads_per_step(H, col0, 4)
    w = hps * ATT_HEAD
    c0 = col0 // w
    hb = H // hps
    new = lambda off: pl.BlockSpec((tseq, w), lambda b, h: (b, c0 + off + h))
    cache = pl.BlockSpec((None, L * H, ATT_HEAD), lambda b, h: (b, 0, 0))
    gspec = pl.BlockSpec((1, ATT_HEAD), lambda b, h: (0, 0))
    out = pl.BlockSpec((tseq, w), lambda b, h: (b, h))
    return pl.pallas_call(
        functools.partial(_cached_attn_kernel, t=tseq, hps=hps, nheads=H),
        out_shape=[jax.ShapeDtypeStruct((M, da), BF16),
                   jax.ShapeDtypeStruct((M, da), F32)],
        grid=(nseq, hb),
        in_specs=[new(0), new(hb), new(2 * hb), cache, cache, gspec, gspec,
                  pl.BlockSpec((hps,) + bias_dec.shape[1:], lambda b, h: (h, 0, 0))],
        out_specs=[out, out],
        compiler_params=_cparams("parallel", "arbitrary"),
        name="cached_attention",
    )(proj, proj, proj, k_cache, v_cache, q_g.reshape(1, -1), k_g.reshape(1, -1), bias_dec)


def _outproj_kernel(yr_ref, ya_ref, w_ref, x_ref, h_ref, *, dr):
    acc = jnp.dot(yr_ref[...], w_ref[:dr, :], preferred_element_type=F32)
    acc += jnp.dot(ya_ref[...], w_ref[dr:, :], preferred_element_type=F32)
    h_ref[...] = x_ref[...] + acc


def _outproj(y_r, y_a, w_out, x):
    M, D = x.shape
    dr, da = y_r.shape[1], y_a.shape[1]
    tm = _pick_tile(M, (1024, 512, 256, 128, 64, 32))
    tn = _pick_tile(D, (512, 256, 128))
    return pl.pallas_call(
        functools.partial(_outproj_kernel, dr=dr),
        out_shape=jax.ShapeDtypeStruct((M, D), F32),
        grid=(M // tm, D // tn),
        in_specs=[
            pl.BlockSpec((tm, dr), lambda m, n: (m, 0)),
            pl.BlockSpec((tm, da), lambda m, n: (m, 0)),
            pl.BlockSpec((dr + da, tn), lambda m, n: (0, n)),
            pl.BlockSpec((tm, tn), lambda m, n: (m, n)),
        ],
        out_specs=pl.BlockSpec((tm, tn), lambda m, n: (m, n)),
        compiler_params=_cparams("parallel", "arbitrary"),
        name="out_proj",
    )(y_r, y_a, w_out, x)


def _ffn_up_kernel(h_ref, g_ref, wg_ref, wv_ref, cw_ref, cb_ref, b1_ref, b2_ref,
                   act_ref, gate_ref, hn_ref, carry_ref, *, tseq, blocks_per_seq, nsplit):
    m = pl.program_id(0)
    n = pl.program_id(1)

    @pl.when(n == 0)
    def _():
        _rms_to_bf16(h_ref, g_ref, hn_ref)

    tm = hn_ref.shape[0]
    hs = tm // nsplit
    tn = wg_ref.shape[1]
    row8 = lax.broadcasted_iota(jnp.int32, (SUBLANES, tn), 0)
    gate = None
    for c in range(nsplit):
        rows = slice(c * hs, (c + 1) * hs)
        hn = hn_ref[rows, :]
        prev_gate = gate
        gate = jnp.dot(hn, wg_ref[...], preferred_element_type=F32)
        val = jnp.dot(hn, wv_ref[...], preferred_element_type=F32)
        r1 = pltpu.roll(gate, 1, axis=0)
        r2 = pltpu.roll(gate, 2, axis=0)
        if blocks_per_seq >= 1:
            if c == 0:
                first = m % blocks_per_seq == 0
                prev = carry_ref[n]
                p2 = jnp.where(first, b2_ref[0:1, :], prev[SUBLANES - 2:SUBLANES - 1, :])
                p1 = jnp.where(first, b1_ref[0:1, :], prev[SUBLANES - 1:SUBLANES, :])
            else:
                p2 = prev_gate[hs - 2:hs - 1, :]
                p1 = prev_gate[hs - 1:hs, :]
            top1 = jnp.where(row8 == 0, p1, r1[:SUBLANES])
            top2 = jnp.where(row8 == 0, p2, jnp.where(row8 == 1, p1, r2[:SUBLANES]))
            g1 = jnp.concatenate([top1, r1[SUBLANES:]], axis=0)
            g2 = jnp.concatenate([top2, r2[SUBLANES:]], axis=0)
        else:
            t = lax.broadcasted_iota(jnp.int32, gate.shape, 0) % tseq
            g1 = jnp.where(t == 0, b1_ref[rows, :], r1)
            g2 = jnp.where(t < 2, b2_ref[rows, :], r2)
            gate_ref[rows, :] = gate
        conv = cb_ref[...] + g2 * cw_ref[0:1, :] + g1 * cw_ref[1:2, :] + gate * cw_ref[2:3, :]
        act_ref[rows, :] = (conv * _sigmoid(conv) * val).astype(BF16)
    if blocks_per_seq >= 1:
        carry_ref[n] = gate[hs - SUBLANES:, :]
        gate_ref[...] = gate[hs - (FFN_CONV - 1):, :]


def _ffn_up(h, g, w_up, conv_w, conv_b, conv_prev, nseq, tseq):
    M, D = h.shape
    Fd = w_up.shape[1] // 2
    tn = _pick_tile(Fd, (256, 128))
    nn = Fd // tn
    nsplit = 2
    if tseq >= 512:
        tm = _pick_tile(tseq, (1024, 512))
        bps = tseq // tm
        b1 = conv_prev[:, 1:2, :]
        b2 = conv_prev[:, 0:1, :]
        bspec = pl.BlockSpec((None, 1, tn), lambda m, n: (m // bps, 0, n))
        gate_shape = jax.ShapeDtypeStruct((M // tm, FFN_CONV - 1, Fd), F32)
        gate_spec = pl.BlockSpec((None, FFN_CONV - 1, tn), lambda m, n: (m, 0, n))
    else:
        tm = M
        bps = 0
        z = jnp.zeros((nseq, tseq, Fd), F32)
        b1 = z.at[:, 0].set(conv_prev[:, 1]).reshape(M, Fd)
        b2 = z.at[:, 0].set(conv_prev[:, 0]).at[:, 1].set(conv_prev[:, 1]).reshape(M, Fd)
        bspec = pl.BlockSpec((tm, tn), lambda m, n: (m, n))
        gate_shape = jax.ShapeDtypeStruct((M, Fd), F32)
        gate_spec = pl.BlockSpec((tm, tn), lambda m, n: (m, n))
    act, gate = pl.pallas_call(
        functools.partial(_ffn_up_kernel, tseq=tseq, blocks_per_seq=bps, nsplit=nsplit),
        out_shape=[jax.ShapeDtypeStruct((M, Fd), BF16), gate_shape],
        grid=(M // tm, nn),
        in_specs=[
            pl.BlockSpec((tm, D), lambda m, n: (m, 0), pipeline_mode=pl.Buffered(1)),
            pl.BlockSpec((1, D), lambda m, n: (0, 0)),
            pl.BlockSpec((D, tn), lambda m, n: (0, n)),
            pl.BlockSpec((D, tn), lambda m, n: (0, nn + n)),
            pl.BlockSpec((FFN_CONV, tn), lambda m, n: (0, n)),
            pl.BlockSpec((1, tn), lambda m, n: (0, n)),
            bspec, bspec,
        ],
        out_specs=[pl.BlockSpec((tm, tn), lambda m, n: (m, n)), gate_spec],
        scratch_shapes=[pltpu.VMEM((tm, D), BF16), pltpu.VMEM((nn, SUBLANES, tn), F32)],
        compiler_params=_cparams("arbitrary", "arbitrary"),
        name="ffn_up",
    )(h, g.reshape(1, D), w_up, w_up, conv_w, conv_b.reshape(1, Fd), b1, b2)
    if bps == 0:
        gate = gate.reshape(nseq, tseq, Fd)[:, tseq - (FFN_CONV - 1):]
    else:
        gate = gate[bps - 1::bps]
    return act, gate


def _ffn_down_kernel(a_ref, w_ref, h_ref, o_ref):
    o_ref[...] = h_ref[...] + jnp.dot(a_ref[...], w_ref[...], preferred_element_type=F32)


def _ffn_down(act, w_down, h):
    M, Fd = act.shape
    D = w_down.shape[1]
    tm = _pick_tile(M, (512, 256, 128, 64, 32))
    tn = _pick_tile(D, (256, 128))
    return pl.pallas_call(
        _ffn_down_kernel,
        out_shape=jax.ShapeDtypeStruct((M, D), F32),
        grid=(M // tm, D // tn),
        in_specs=[
            pl.BlockSpec((tm, Fd), lambda m, n: (m, 0)),
            pl.BlockSpec((Fd, tn), lambda m, n: (0, n)),
            pl.BlockSpec((tm, tn), lambda m, n: (m, n)),
        ],
        out_specs=pl.BlockSpec((tm, tn), lambda m, n: (m, n)),
        compiler_params=_cparams("parallel", "arbitrary"),
        name="ffn_down",
    )(act, w_down, h)


def _layer_weights(lp):
    (attn_norm_g, w_in, mu, w0, w2, a0, a2, g2, k_k, k_a, r_k, ln_w, ln_b, q_g, k_g, rel_bias,
     w_out, ffn_norm_g, w_up, conv_w, conv_b, w_down) = lp
    D = w_in.shape[0]
    dr = w0.shape[0]
    dl, al, gl = w2.shape[0], a2.shape[0], g2.shape[0]
    nrc = 3 * dr + dl + al + gl
    lpad = _round_up(dl + al + gl, LANES)
    cr = 3 * dr + lpad
    assert cr <= w_in.shape[1]
    w_in_r = w_in[:, :cr].astype(BF16)
    w_in_a = w_in[:, nrc:].astype(BF16)
    mu_p = jnp.concatenate([mu, jnp.zeros((cr - nrc,), mu.dtype)]).reshape(1, cr)
    zrow = lambda n: jnp.zeros((n, dr), F32)
    w2p = jnp.concatenate([w2, zrow(lpad - dl)], axis=0).astype(BF16)
    a2p = jnp.concatenate([zrow(dl), a2, zrow(lpad - dl - al)], axis=0).astype(BF16)
    g2p = jnp.concatenate([zrow(dl + al), g2, zrow(lpad - dl - al - gl)], axis=0).astype(BF16)
    return dict(attn_norm_g=attn_norm_g, w_in_r=w_in_r, w_in_a=w_in_a, mu_p=mu_p, w0=w0, a0=a0, k_k=k_k, k_a=k_a,
                r_k=r_k.reshape(-1), w2p=w2p, a2p=a2p, g2p=g2p, ln_w=ln_w, ln_b=ln_b, q_g=q_g, k_g=k_g,
                rel_bias=rel_bias, w_out=w_out.astype(BF16), ffn_norm_g=ffn_norm_g,
                w_up=w_up.astype(BF16), conv_w=conv_w, conv_b=conv_b, w_down=w_down.astype(BF16),
                nrc=nrc, cr=cr, dr=dr)


def _block(x, shift_prev, wkv_prev, conv_prev, k_cache, v_cache, W, bias_band, bias_dec):
    B, T, D = x.shape
    M = B * T
    dr, cr, nrc = W["dr"], W["cr"], W["nrc"]
    H = (D - dr) // ATT_HEAD
    xf = x.reshape(M, D)
    proj = _rms_matmul(xf, W["attn_norm_g"], W["w_in_r"], W["w_in_a"])

    shift_p = jnp.concatenate([shift_prev, jnp.zeros((B, 1, cr - nrc), F32)], axis=-1)
    prep = _rwkv_prep(proj, shift_p, B, T, cr, dr, W["mu_p"], W["w0"], W["a0"], W["k_k"], W["k_a"],
                      W["r_k"], W["w2p"], W["a2p"], W["g2p"])
    y_r, wkv_new = _wkv(prep, wkv_prev, B, T, dr, W["ln_w"], W["ln_b"])
    shift_new = proj.reshape(B, T, -1)[:, T - 1:, :nrc]

    da = H * ATT_HEAD
    if k_cache is None:
        y_a, k_new = _band_attention(proj, cr, B, T, H, W["q_g"], W["k_g"], bias_band)
        keep = min(ATT_CTX, T)
        k_new = k_new.reshape(B, keep, H, ATT_HEAD)
        v_new = proj.reshape(B, T, -1)[:, T - keep:, cr + 2 * da:].reshape(B, keep, H, ATT_HEAD)
    else:
        L = k_cache.shape[1]
        y_a, k_new = _cached_attention(proj, cr, B, T, H, k_cache.reshape(B, L * H, ATT_HEAD),
                                       v_cache.reshape(B, L * H, ATT_HEAD), W["q_g"], W["k_g"], bias_dec)
        k_new = k_new.reshape(B, T, H, ATT_HEAD)
        v_new = proj.reshape(B, T, -1)[:, :, cr + 2 * da:].reshape(B, T, H, ATT_HEAD)

    h = _outproj(y_r, y_a, W["w_out"], xf)
    act, conv_new = _ffn_up(h, W["ffn_norm_g"], W["w_up"], W["conv_w"], W["conv_b"], conv_prev, B, T)
    y = _ffn_down(act, W["w_down"], h)
    return y.reshape(B, T, D), shift_new, wkv_new, k_new, v_new, conv_new


def kernel(x_prompt, x_sample, state_rwkv_shift, state_rwkv_wkv, cache_attn_k, cache_attn_v,
           state_ffn_conv, attn_norm_g, w_in, rwkv_mu, rwkv_w0, rwkv_w2, rwkv_a0, rwkv_a2, rwkv_g2,
           rwkv_k_k, rwkv_k_a, rwkv_r_k, rwkv_ln_w, rwkv_ln_b, att_q_norm_g, att_k_norm_g,
           att_rel_bias, w_out, ffn_norm_g, ffn_w_up, ffn_conv_w, ffn_conv_b, ffn_w_down):
    depth = w_in.shape[0]
    B = x_prompt.shape[0]
    xp, xs = x_prompt, x_sample
    dec_t = x_sample.shape[1]
    L = cache_attn_k.shape[2]
    assert L == ATT_CTX
    dec_w = _round_up(L + dec_t, LANES)
    p_states = ([], [], [], [], [])
    s_states = ([], [], [], [], [])
    for l in range(depth):
        lp = (attn_norm_g[l], w_in[l], rwkv_mu[l], rwkv_w0[l], rwkv_w2[l], rwkv_a0[l], rwkv_a2[l],
              rwkv_g2[l], rwkv_k_k[l], rwkv_k_a[l], rwkv_r_k[l], rwkv_ln_w[l], rwkv_ln_b[l],
              att_q_norm_g[l], att_k_norm_g[l], att_rel_bias[l], w_out[l], ffn_norm_g[l],
              ffn_w_up[l], ffn_conv_w[l], ffn_conv_b[l], ffn_w_down[l])
        W = _layer_weights(lp)
        nrc, dr = W["nrc"], W["dr"]
        Fd = ffn_conv_b.shape[-1]
        bias_band, bias_dec = _bias_tiles(att_rel_bias[l], ATT_CTX, dec_t, dec_w)
        shift0 = jnp.zeros((B, 1, nrc), F32)
        wkv0 = jnp.zeros((B, dr // RWKV_HEAD, RWKV_HEAD, RWKV_HEAD), F32)
        conv0 = jnp.zeros((B, FFN_CONV - 1, Fd), F32)
        xp, *ps = _block(xp, shift0, wkv0, conv0, None, None, W, bias_band, bias_dec)
        xs, *ss = _block(xs, state_rwkv_shift[l], state_rwkv_wkv[l], state_ffn_conv[l],
                         cache_attn_k[l], cache_attn_v[l], W, bias_band, bias_dec)
        for i in range(5):
            p_states[i].append(ps[i])
            s_states[i].append(ss[i])
    p_out = [jnp.stack(t, axis=0) for t in p_states]
    s_out = [jnp.stack(t, axis=0) for t in s_states]
    return (xp, xs, *p_out, *s_out)
```

```python
import functools
import math

import jax
import jax.numpy as jnp
from jax import lax
from jax.experimental import pallas as pl
from jax.experimental.pallas import tpu as pltpu

F32 = jnp.float32
BF16 = jnp.bfloat16

RMS_EPS = 1e-6
GN_EPS = 64e-5
NEG_INF = -1e30
CHUNK = 64
N_PAST_CHUNKS = 8
ATT_CTX = N_PAST_CHUNKS * CHUNK
REL_CLIP = 128
RWKV_HEAD = 64
ATT_HEAD = 128
FFN_CONV = 3

LANES = 128
SUBLANES = 8
VMEM_LIMIT = 56 * 1024 * 1024
TOEP_W = 2048
RMS_ROW_CHUNK = 256


def _cparams(*sem):
    return pltpu.CompilerParams(dimension_semantics=sem, vmem_limit_bytes=VMEM_LIMIT)


def _round_up(n, m):
    return (n + m - 1) // m * m


def _pick_tile(n, candidates):
    for c in candidates:
        if n % c == 0:
            return c
    raise ValueError(f"no tile in {candidates} divides {n}")


def _pair_ones():
    r = lax.broadcasted_iota(jnp.int32, (LANES, LANES), 0) // RWKV_HEAD
    c = lax.broadcasted_iota(jnp.int32, (LANES, LANES), 1) // RWKV_HEAD
    return (r == c).astype(BF16)


def _seg_sum(x, ones):
    hi = x.astype(BF16)
    lo = (x - hi.astype(F32)).astype(BF16)
    return (jnp.dot(hi, ones, preferred_element_type=F32)
            + jnp.dot(lo, ones, preferred_element_type=F32))


def _seg_sum_wide(x, ones):
    rows, width = x.shape
    blocks = [slice(c * LANES, (c + 1) * LANES) for c in range(width // LANES)]
    r = _seg_sum(jnp.concatenate([x[:, sl] for sl in blocks], axis=0), ones)
    return jnp.concatenate([r[c * rows:(c + 1) * rows] for c in range(len(blocks))], axis=1)


def _sigmoid(z):
    return 1.0 / (1.0 + jnp.exp(-z))


def _rms_to_bf16(x_ref, g_ref, xn_ref):
    tm = x_ref.shape[0]
    rc = min(tm, RMS_ROW_CHUNK)
    for c in range(tm // rc):
        rows = slice(c * rc, (c + 1) * rc)
        x = x_ref[rows, :]
        ms = jnp.mean(x * x, axis=-1, keepdims=True)
        xn_ref[rows, :] = (x * lax.rsqrt(ms + RMS_EPS) * g_ref[...]).astype(BF16)


def _rms_matmul_kernel(x_ref, g_ref, wa_ref, wb_ref, o_ref, xn_ref, *, na):
    n = pl.program_id(1)

    @pl.when(n == 0)
    def _():
        _rms_to_bf16(x_ref, g_ref, xn_ref)

    @pl.when(n < na)
    def _():
        o_ref[...] = jnp.dot(xn_ref[...], wa_ref[...], preferred_element_type=F32)

    @pl.when(n >= na)
    def _():
        o_ref[...] = jnp.dot(xn_ref[...], wb_ref[...], preferred_element_type=F32)


def _rms_matmul(x, g, wa, wb):
    M, D = x.shape
    Na, Nb = wa.shape[1], wb.shape[1]
    tm = _pick_tile(M, (1024, 512, 256, 128, 64, 32))
    tn = _pick_tile(math.gcd(Na, Nb), (512, 256, 128))
    na, nb = Na // tn, Nb // tn
    return pl.pallas_call(
        functools.partial(_rms_matmul_kernel, na=na),
        out_shape=jax.ShapeDtypeStruct((M, Na + Nb), F32),
        grid=(M // tm, na + nb),
        in_specs=[
            pl.BlockSpec((tm, D), lambda m, n: (m, 0), pipeline_mode=pl.Buffered(1)),
            pl.BlockSpec((1, D), lambda m, n: (0, 0)),
            pl.BlockSpec((D, tn), lambda m, n: (0, jnp.minimum(n, na - 1))),
            pl.BlockSpec((D, tn), lambda m, n: (0, jnp.maximum(n - na, 0))),
        ],
        out_specs=pl.BlockSpec((tm, tn), lambda m, n: (m, n)),
        scratch_shapes=[pltpu.VMEM((tm, D), BF16)],
        compiler_params=_cparams("parallel", "arbitrary"),
        name="rms_inproj",
    )(x, g.reshape(1, D), wa, wb)


def _rwkv_prep_kernel(p_ref, prev_ref, mu_ref, w0_ref, a0_ref, kk_ref, ka_ref, rk_ref,
                      w2_ref, a2_ref, g2_ref,
                      a_out, b_out, w_out, k_out, rp_out, v_out, g_out, kr_out, bon_out,
                      last_ref, *, dr):
    @pl.when(pl.program_id(1) == 0)
    def _():
        last_ref[...] = prev_ref[...]

    p = p_ref[...]
    tb = p.shape[0]
    row = lax.broadcasted_iota(jnp.int32, p.shape, 0)
    shifted = jnp.where(row == 0, last_ref[...], pltpu.roll(p, 1, axis=0))
    last_ref[...] = p[tb - 1:tb, :]
    xm = p + (shifted - p) * mu_ref[...]

    lo = xm[:, 3 * dr:]
    wl = jnp.dot(jnp.tanh(lo).astype(BF16), w2_ref[...], preferred_element_type=F32)
    al = jnp.dot(lo.astype(BF16), a2_ref[...], preferred_element_type=F32)
    g_out[...] = jnp.dot(_sigmoid(lo).astype(BF16), g2_ref[...], preferred_element_type=F32)

    ones = _pair_ones()
    r = xm[:, :dr]
    k = xm[:, dr:2 * dr]
    v_out[...] = xm[:, 2 * dr:3 * dr]
    decay = jnp.exp(-math.exp(-0.5) * _sigmoid(w0_ref[...] + wl))
    asig = _sigmoid(a0_ref[...] + al)
    kk = k * kk_ref[...]
    kk = kk * lax.rsqrt(jnp.maximum(_seg_sum_wide(kk * kk, ones), 1e-24))
    k2 = k * (1.0 + (asig - 1.0) * ka_ref[...])
    b_s = kk * asig
    rk2 = r * k2
    a_out[...] = -kk
    b_out[...] = b_s
    w_out[...] = decay
    k_out[...] = k2
    rp_out[...] = decay * r - kk * _seg_sum_wide(b_s * r, ones)
    kr_out[...] = _seg_sum_wide(rk2, ones)
    bon_out[...] = _seg_sum_wide(rk2 * rk_ref[...], ones)


def _rwkv_prep(proj, shift_prev, nseq, tseq, cr, dr, mu_p, w0, a0, k_k, k_a, r_k, w2p, a2p, g2p):
    M = proj.shape[0]
    tb = _pick_tile(tseq, (128, 64, 32))
    nt = tseq // tb
    lp = cr - 3 * dr
    row = lambda a: a.reshape(1, -1).astype(F32)
    vec_spec = pl.BlockSpec((1, dr), lambda b, i: (0, 0))
    mat_spec = pl.BlockSpec((lp, dr), lambda b, i: (0, 0))
    out_spec = pl.BlockSpec((tb, dr), lambda b, i: (b * nt + i, 0))
    outs = pl.pallas_call(
        functools.partial(_rwkv_prep_kernel, dr=dr),
        out_shape=[jax.ShapeDtypeStruct((M, dr), F32)] * 9,
        grid=(nseq, nt),
        in_specs=[
            pl.BlockSpec((tb, cr), lambda b, i: (b * nt + i, 0)),
            pl.BlockSpec((None, 1, cr), lambda b, i: (b, 0, 0)),
            pl.BlockSpec((1, cr), lambda b, i: (0, 0)),
            vec_spec, vec_spec, vec_spec, vec_spec, vec_spec,
            mat_spec, mat_spec, mat_spec,
        ],
        out_specs=[out_spec] * 9,
        scratch_shapes=[pltpu.VMEM((1, cr), F32)],
        compiler_params=_cparams("parallel", "arbitrary"),
        name="rwkv_prep",
    )(proj, shift_prev, mu_p, row(w0), row(a0), row(k_k), row(k_a), row(r_k), w2p, a2p, g2p)
    return outs


def _pair_transpose(x):
    t2 = jnp.concatenate([x, x], axis=0).T
    lane = lax.broadcasted_iota(jnp.int32, x.shape, 1)
    return jnp.where(lane < RWKV_HEAD, t2[:RWKV_HEAD], t2[RWKV_HEAD:])


def _stack_heads(x):
    lane = lax.broadcasted_iota(jnp.int32, x.shape, 1)
    return jnp.concatenate([jnp.where(lane < RWKV_HEAD, x, 0.0),
                            jnp.where(lane >= RWKV_HEAD, x, 0.0)], axis=0)


def _wkv_kernel(a_ref, b_ref, w_ref, k_ref, rp_ref, v_ref, g_ref, kr_ref, bon_ref,
                s0_ref, lnw_ref, lnb_ref, y_ref, s_out_ref,
                s_scr, vt_scr, ut_scr, y0_scr, vd_scr, cf_scr, df_scr, kf_scr,
                at_scr, bt_scr, cw_scr, *, tb, npairs, nb):
    i = pl.program_id(1)
    n = RWKV_HEAD

    units = [(bi, p) for bi in range(nb) for p in range(npairs)]
    nu = len(units)
    col = lambda p: slice(p * LANES, (p + 1) * LANES)

    @pl.when(i == 0)
    def _():
        for q, (bi, p) in enumerate(units):
            s_scr[q] = jnp.concatenate([s0_ref[bi, 2 * p], s0_ref[bi, 2 * p + 1]], axis=1)

    ones = _pair_ones()
    s_lt_j = (lax.broadcasted_iota(jnp.int32, (LANES, LANES), 0) % n
              < lax.broadcasted_iota(jnp.int32, (LANES, LANES), 1) % n)

    def pad_rows(x):
        return x if tb == n else jnp.concatenate([x, jnp.zeros((n - tb, LANES), F32)], axis=0)

    trow_all = lax.broadcasted_iota(jnp.int32, w_ref.shape[1:], 0)
    kt_all, rho_all = [], []
    for bi in range(nb):
        cw = w_ref[bi]
        shift = 1
        while shift < tb:
            cw = jnp.where(trow_all >= shift, cw * pltpu.roll(cw, shift, axis=0), cw)
            shift *= 2
        cw_prev = jnp.where(trow_all == 0, 1.0, pltpu.roll(cw, 1, axis=0))
        inv = 1.0 / cw
        at_scr[bi] = a_ref[bi] * cw_prev
        bt_scr[bi] = b_ref[bi] * inv
        cw_scr[bi] = cw[tb - 1:tb]
        kt_all.append(k_ref[bi] * inv)
        rho_all.append(rp_ref[bi] * cw_prev)

    stacked = lambda x: _stack_heads(pad_rows(x)).astype(BF16)
    rhos = [stacked(rho_all[bi][:, col(p)]) for bi, p in units]
    afs = [stacked(at_scr[bi, :, col(p)]) for bi, p in units]
    bfs = [stacked(bt_scr[bi, :, col(p)]) for bi, p in units]
    kfs = [stacked(kt_all[bi][:, col(p)]) for bi, p in units]
    r1s = [_dot_nt(jnp.concatenate([bfs[q], kfs[q], s_scr[q].astype(BF16)], axis=0), rhos[q])
           for q in range(nu)]
    kas = [_dot_nt(kfs[q], afs[q]) for q in range(nu)]
    vts = [_pair_transpose(pad_rows(v_ref[bi, :, col(p)])).astype(BF16) for bi, p in units]
    for q in range(nu):
        vt_scr[q] = vts[q]
        kf_scr[q] = kfs[q]
        ut_scr[q] = jnp.zeros((n, LANES), F32)
        cf_scr[q] = jnp.where(s_lt_j, r1s[q][:LANES], 0.0).astype(BF16)
        df_scr[q] = jnp.where(s_lt_j, r1s[q][LANES:2 * LANES], 0.0).astype(BF16)
        y0_scr[q] = r1s[q][2 * LANES:]
    kas = [jnp.where(s_lt_j, ka, 0.0).astype(BF16) for ka in kas]
    for q in range(nu):
        vd_scr[q] = jnp.dot(vts[q], kas[q], preferred_element_type=F32).astype(BF16)

    lane = lax.broadcasted_iota(jnp.int32, (n, LANES), 1) % n

    gsize = 8 if nu % 8 == 0 else nu
    ngroups = nu // gsize

    def steps(t8, carry):
        rows = pl.ds(pl.multiple_of(t8 * SUBLANES, SUBLANES), SUBLANES)
        for j in range(SUBLANES):
            mask = lane == t8 * SUBLANES + j
            row = lambda ref, q: ref[units[q][0], rows, col(units[q][1])][j:j + 1]
            for g in range(ngroups):
                qs = range(g * gsize, (g + 1) * gsize)
                lhs = jnp.concatenate(
                    [(s_scr[q] * row(at_scr, q)).astype(BF16)
                     + jnp.where(mask, vd_scr[q], jnp.zeros((), BF16)) for q in qs], axis=0)
                res = jnp.dot(lhs, ones, preferred_element_type=F32)
                for x, q in enumerate(qs):
                    u = res[x * n:(x + 1) * n]
                    s_scr[q] = s_scr[q] + u * row(bt_scr, q)
                    pltpu.store(ut_scr.at[q], u, mask=mask)
        return carry

    lax.fori_loop(0, tb // SUBLANES, steps, 0)

    inv_n = 1.0 / n

    def transposed_outputs(bi):
        ys = []
        for p in range(npairs):
            q = bi * npairs + p
            vt = vt_scr[q]
            z_end = s_scr[q] + jnp.dot(vt, kf_scr[q], preferred_element_type=F32)
            s_scr[q] = z_end * cw_scr[bi, :, col(p)]
            yt = (y0_scr[q]
                  + jnp.dot(ut_scr[q].astype(BF16), cf_scr[q], preferred_element_type=F32)
                  + jnp.dot(vt, df_scr[q], preferred_element_type=F32))
            ys.append(_pair_transpose(yt)[:tb])
        return jnp.concatenate(ys, axis=0)

    def normalise_and_gate(bi, y):
        stack = lambda ref: jnp.concatenate([ref[bi, :, col(p)] for p in range(npairs)], axis=0)
        v_all = stack(v_ref)
        y = y + v_all * stack(kr_ref)
        mean = _seg_sum(y, ones) * inv_n
        d = y - mean
        var = _seg_sum(d * d, ones) * inv_n
        yn = d * lax.rsqrt(var + GN_EPS)
        bonus = stack(bon_ref) * v_all
        for p in range(npairs):
            rows_p = slice(p * tb, (p + 1) * tb)
            out = (yn[rows_p] * lnw_ref[:, col(p)] + lnb_ref[:, col(p)] + bonus[rows_p]) * g_ref[bi, :, col(p)]
            y_ref[bi, :, col(p)] = out.astype(BF16)

    y_prev = None
    for bi in range(nb):
        y_cur = transposed_outputs(bi)
        if y_prev is not None:
            normalise_and_gate(bi - 1, y_prev)
        y_prev = y_cur
    normalise_and_gate(nb - 1, y_prev)

    @pl.when(i == pl.num_programs(1) - 1)
    def _():
        for q, (bi, p) in enumerate(units):
            s = s_scr[q]
            s_out_ref[bi, 2 * p] = s[:, :n]
            s_out_ref[bi, 2 * p + 1] = s[:, n:]


def _wkv(prep, s0, nseq, tseq, dr, ln_w, ln_b):
    M = prep[0].shape[0]
    tb = min(tseq, RWKV_HEAD)
    nt = tseq // tb
    npairs = dr // LANES
    nb = 2
    assert nseq % nb == 0
    nu = nb * npairs
    prep = [a.reshape(nseq, tseq, dr) for a in prep]
    blk = pl.BlockSpec((nb, tb, dr), lambda b, i: (b, i, 0))
    st = pl.BlockSpec((nb, 2 * npairs, RWKV_HEAD, RWKV_HEAD), lambda b, i: (b, 0, 0, 0))
    vec = pl.BlockSpec((1, dr), lambda b, i: (0, 0))
    y, s_new = pl.pallas_call(
        functools.partial(_wkv_kernel, tb=tb, npairs=npairs, nb=nb),
        out_shape=[jax.ShapeDtypeStruct((nseq, tseq, dr), BF16),
                   jax.ShapeDtypeStruct(s0.shape, F32)],
        grid=(nseq // nb, nt),
        in_specs=[blk] * 9 + [st, vec, vec],
        out_specs=[blk, st],
        scratch_shapes=[pltpu.VMEM((nu, RWKV_HEAD, LANES), F32),
                        pltpu.VMEM((nu, RWKV_HEAD, LANES), BF16),
                        pltpu.VMEM((nu, RWKV_HEAD, LANES), F32),
                        pltpu.VMEM((nu, RWKV_HEAD, LANES), F32),
                        pltpu.VMEM((nu, RWKV_HEAD, LANES), BF16),
                        pltpu.VMEM((nu, LANES, LANES), BF16),
                        pltpu.VMEM((nu, LANES, LANES), BF16),
                        pltpu.VMEM((nu, LANES, LANES), BF16),
                        pltpu.VMEM((nb, tb, dr), F32), pltpu.VMEM((nb, tb, dr), F32),
                        pltpu.VMEM((nb, 1, dr), F32)],
        compiler_params=_cparams("parallel", "arbitrary"),
        name="wkv7",
    )(*prep, s0, ln_w.reshape(1, dr).astype(F32), ln_b.reshape(1, dr).astype(F32))
    return y.reshape(M, dr), s_new


def _bias_kernel(tab_ref, band_ref, dec_ref, *, nrel, qb, dec_t, dec_w):
    h = pl.program_id(0)
    m = lax.broadcasted_iota(jnp.int32, (SUBLANES, TOEP_W), 1)
    dd = jnp.where(m < TOEP_W // 2, m, m - TOEP_W)
    idx = jnp.clip(dd - ATT_CTX, -REL_CLIP, REL_CLIP) + REL_CLIP

    def pick(r, acc):
        return jnp.where(idx == r, tab_ref[h * nrel + r], acc)

    e = lax.fori_loop(0, nrel, pick, jnp.zeros((SUBLANES, TOEP_W), F32))
    eb = jnp.broadcast_to(e[0:1], (qb, TOEP_W))
    toe = pltpu.roll(eb, 0, axis=1, stride=1, stride_axis=0)
    kb = band_ref.shape[-1]
    cq = lax.broadcasted_iota(jnp.int32, (qb, kb), 0) // CHUNK
    ck = lax.broadcasted_iota(jnp.int32, (qb, kb), 1) // CHUNK
    valid = (ck >= cq) & (ck <= cq + N_PAST_CHUNKS)
    band_ref[...] = jnp.where(valid, toe[:, :kb], NEG_INF)
    dec_ref[...] = toe[:dec_t, :dec_w]


def _bias_tiles(table, qb, dec_t, dec_w):
    nrel, H = table.shape
    tab = table.T.reshape(-1).astype(F32)
    return pl.pallas_call(
        functools.partial(_bias_kernel, nrel=nrel, qb=qb, dec_t=dec_t, dec_w=dec_w),
        out_shape=[jax.ShapeDtypeStruct((H, qb, qb + ATT_CTX), F32),
                   jax.ShapeDtypeStruct((H, dec_t, dec_w), F32)],
        grid=(H,),
        in_specs=[pl.BlockSpec(memory_space=pltpu.SMEM)],
        out_specs=[pl.BlockSpec((None, qb, qb + ATT_CTX), lambda h: (h, 0, 0)),
                   pl.BlockSpec((None, dec_t, dec_w), lambda h: (h, 0, 0))],
        compiler_params=_cparams("arbitrary"),
        name="rel_bias_tiles",
    )(tab)


def _head_rms(x, g):
    return x * lax.rsqrt(jnp.mean(x * x, axis=-1, keepdims=True) + RMS_EPS) * g


def _dot_nt(a, b):
    return lax.dot_general(a, b, (((1,), (1,)), ((), ())), preferred_element_type=F32)


def _heads_per_step(nheads, col0, most):
    for hps in (4, 2, 1):
        if hps <= most and nheads % hps == 0 and col0 % (hps * ATT_HEAD) == 0:
            return hps
    raise ValueError("attention columns are not head aligned")


def _band_attn_kernel(q_ref, kc_ref, kp_ref, vc_ref, vp_ref, qg_ref, kg_ref, bias_ref,
                      o_ref, knew_ref, *, hps):
    i = pl.program_id(2)
    scale = ATT_HEAD ** -0.5
    qb = q_ref.shape[0]
    qs = 2 * CHUNK
    ks = qs + ATT_CTX
    kidx = lax.broadcasted_iota(jnp.int32, (qs, ks), 1)
    heads = [slice(hh * ATT_HEAD, (hh + 1) * ATT_HEAD) for hh in range(hps)]
    qn = [(_head_rms(q_ref[:, sl], qg_ref[...]) * scale).astype(BF16) for sl in heads]
    kc = [_head_rms(kc_ref[:, sl], kg_ref[...]) for sl in heads]
    kp = [_head_rms(kp_ref[:, sl], kg_ref[...]) for sl in heads]
    k_all = [jnp.concatenate([kp[hh], kc[hh]], axis=0).astype(BF16) for hh in range(hps)]
    v_all = [jnp.concatenate([vp_ref[:, sl], vc_ref[:, sl]], axis=0).astype(BF16) for sl in heads]
    items = [(hh, r * qs) for hh in range(hps) for r in range(qb // qs)]
    s = [_dot_nt(qn[hh][r0:r0 + qs], k_all[hh][r0:r0 + ks]) + bias_ref[hh, r0:r0 + qs, r0:r0 + ks]
         for hh, r0 in items]
    s = [jnp.where((i > 0) | (kidx >= ATT_CTX - r0), x, NEG_INF) for x, (hh, r0) in zip(s, items)]
    m = [jnp.max(x, axis=-1, keepdims=True) for x in s]
    p = [jnp.exp(x - y) for x, y in zip(s, m)]
    l = [jnp.sum(x, axis=-1, keepdims=True) for x in p]
    o = [jnp.dot(x.astype(BF16), v_all[hh][r0:r0 + ks], preferred_element_type=F32)
         for x, (hh, r0) in zip(p, items)]
    for x, y, (hh, r0) in zip(o, l, items):
        o_ref[r0:r0 + qs, heads[hh]] = (x / y).astype(BF16)

    @pl.when(i == pl.num_programs(2) - 1)
    def _():
        for hh in range(hps):
            knew_ref[:, heads[hh]] = kc[hh]


def _band_attention(proj, col0, nseq, tseq, nheads, q_g, k_g, bias_band):
    M = proj.shape[0]
    qb = ATT_CTX
    assert tseq % qb == 0
    nb = tseq // qb
    H = nheads
    hps = _heads_per_step(H, col0, 4)
    w = hps * ATT_HEAD
    c0 = col0 // w
    hb = H // hps
    cur = lambda off: pl.BlockSpec((qb, w), lambda b, h, i: (b * nb + i, c0 + off + h))
    prv = lambda off: pl.BlockSpec(
        (qb, w), lambda b, h, i: (b * nb + jnp.maximum(i - 1, 0), c0 + off + h))
    gspec = pl.BlockSpec((1, ATT_HEAD), lambda b, h, i: (0, 0))
    return pl.pallas_call(
        functools.partial(_band_attn_kernel, hps=hps),
        out_shape=[jax.ShapeDtypeStruct((M, H * ATT_HEAD), BF16),
                   jax.ShapeDtypeStruct((nseq, qb, H * ATT_HEAD), F32)],
        grid=(nseq, hb, nb),
        in_specs=[cur(0), cur(hb), prv(hb), cur(2 * hb), prv(2 * hb), gspec, gspec,
                  pl.BlockSpec((hps, qb, qb + ATT_CTX), lambda b, h, i: (h, 0, 0))],
        out_specs=[pl.BlockSpec((qb, w), lambda b, h, i: (b * nb + i, h)),
                   pl.BlockSpec((None, qb, w), lambda b, h, i: (b, 0, h))],
        compiler_params=_cparams("parallel", "parallel", "arbitrary"),
        name="band_attention",
    )(proj, proj, proj, proj, proj, q_g.reshape(1, -1), k_g.reshape(1, -1), bias_band)


def _cached_attn_kernel(q_ref, kn_ref, vn_ref, kc_ref, vc_ref, qg_ref, kg_ref, bias_ref,
                        o_ref, knew_ref, *, t, hps, nheads):
    scale = ATT_HEAD ** -0.5
    L = kc_ref.shape[0] // nheads
    h0 = pl.program_id(1) * hps
    heads = [slice(h * ATT_HEAD, (h + 1) * ATT_HEAD) for h in range(hps)]
    rows = [pl.ds(h0 + h, L, stride=nheads) for h in range(hps)]
    qn = [(_head_rms(q_ref[:, sl], qg_ref[...]) * scale).astype(BF16) for sl in heads]
    kn = [_head_rms(kn_ref[:, sl], kg_ref[...]) for sl in heads]
    s_c = [_dot_nt(qn[h], kc_ref[rows[h], :].astype(BF16)) + bias_ref[h, :, :L] for h in range(hps)]
    s_n = [_dot_nt(qn[h], kn[h].astype(BF16)) + bias_ref[h, :, L:L + t] for h in range(hps)]
    m = [jnp.maximum(jnp.max(a, axis=-1, keepdims=True), jnp.max(b, axis=-1, keepdims=True))
         for a, b in zip(s_c, s_n)]
    p_c = [jnp.exp(a - y) for a, y in zip(s_c, m)]
    p_n = [jnp.exp(b - y) for b, y in zip(s_n, m)]
    l = [jnp.sum(a, axis=-1, keepdims=True) + jnp.sum(b, axis=-1, keepdims=True)
         for a, b in zip(p_c, p_n)]
    o = [jnp.dot(p_c[h].astype(BF16), vc_ref[rows[h], :].astype(BF16), preferred_element_type=F32)
         + jnp.dot(p_n[h].astype(BF16), vn_ref[:, heads[h]].astype(BF16), preferred_element_type=F32)
         for h in range(hps)]
    for h in range(hps):
        knew_ref[:, heads[h]] = kn[h]
        o_ref[:, heads[h]] = (o[h] / l[h]).astype(BF16)


def _cached_attention(proj, col0, nseq, tseq, nheads, k_cache, v_cache, q_g, k_g, bias_dec):
    M = proj.shape[0]
    H = nheads
    L = k_cache.shape[1] // H
    da = H * ATT_HEAD
    hps = _heads_per_step(H, col0, 4)
    w = hps * ATT_HEAD
    c0 = col0 // w
    hb = H // hps
    new = lambda off: pl.BlockSpec((tseq, w), lambda b, h: (b, c0 + off + h))
    cache = pl.BlockSpec((None, L * H, ATT_HEAD), lambda b, h: (b, 0, 0))
    gspec = pl.BlockSpec((1, ATT_HEAD), lambda b, h: (0, 0))
    out = pl.BlockSpec((tseq, w), lambda b, h: (b, h))
    return pl.pallas_call(
        functools.partial(_cached_attn_kernel, t=tseq, hps=hps, nheads=H),
        out_shape=[jax.ShapeDtypeStruct((M, da), BF16),
                   jax.ShapeDtypeStruct((M, da), F32)],
        grid=(nseq, hb),
        in_specs=[new(0), new(hb), new(2 * hb), cache, cache, gspec, gspec,
                  pl.BlockSpec((hps,) + bias_dec.shape[1:], lambda b, h: (h, 0, 0))],
        out_specs=[out, out],
        compiler_params=_cparams("parallel", "arbitrary"),
        name="cached_attention",
    )(proj, proj, proj, k_cache, v_cache, q_g.reshape(1, -1), k_g.reshape(1, -1), bias_dec)


def _outproj_kernel(yr_ref, ya_ref, w_ref, x_ref, h_ref, *, dr):
    acc = jnp.dot(yr_ref[...], w_ref[:dr, :], preferred_element_type=F32)
    acc += jnp.dot(ya_ref[...], w_ref[dr:, :], preferred_element_type=F32)
    h_ref[...] = x_ref[...] + acc


def _outproj(y_r, y_a, w_out, x):
    M, D = x.shape
    dr, da = y_r.shape[1], y_a.shape[1]
    tm = _pick_tile(M, (1024, 512, 256, 128, 64, 32))
    tn = _pick_tile(D, (512, 256, 128))
    return pl.pallas_call(
        functools.partial(_outproj_kernel, dr=dr),
        out_shape=jax.ShapeDtypeStruct((M, D), F32),
        grid=(M // tm, D // tn),
        in_specs=[
            pl.BlockSpec((tm, dr), lambda m, n: (m, 0)),
            pl.BlockSpec((tm, da), lambda m, n: (m, 0)),
            pl.BlockSpec((dr + da, tn), lambda m, n: (0, n)),
            pl.BlockSpec((tm, tn), lambda m, n: (m, n)),
        ],
        out_specs=pl.BlockSpec((tm, tn), lambda m, n: (m, n)),
        compiler_params=_cparams("parallel", "arbitrary"),
        name="out_proj",
    )(y_r, y_a, w_out, x)


def _ffn_up_kernel(h_ref, g_ref, wg_ref, wv_ref, cw_ref, cb_ref, b1_ref, b2_ref,
                   act_ref, gate_ref, hn_ref, carry_ref, *, tseq, blocks_per_seq, splits):
    m = pl.program_id(0)
    n = pl.program_id(1)

    @pl.when(n == 0)
    def _():
        _rms_to_bf16(h_ref, g_ref, hn_ref)

    tm = hn_ref.shape[0]
    tn = wg_ref.shape[1]
    row8 = lax.broadcasted_iota(jnp.int32, (SUBLANES, tn), 0)
    gate = None
    for c in range(len(splits) - 1):
        rows = slice(splits[c], splits[c + 1])
        hs = splits[c + 1] - splits[c]
        hn = hn_ref[rows, :]
        prev_gate = gate
        gate = jnp.dot(hn, wg_ref[...], preferred_element_type=F32)
        val = jnp.dot(hn, wv_ref[...], preferred_element_type=F32)
        r1 = pltpu.roll(gate, 1, axis=0)
        r2 = pltpu.roll(gate, 2, axis=0)
        if blocks_per_seq >= 1:
            if c == 0:
                first = m % blocks_per_seq == 0
                prev = carry_ref[n]
                p2 = jnp.where(first, b2_ref[0:1, :], prev[SUBLANES - 2:SUBLANES - 1, :])
                p1 = jnp.where(first, b1_ref[0:1, :], prev[SUBLANES - 1:SUBLANES, :])
            else:
                ph = prev_gate.shape[0]
                p2 = prev_gate[ph - 2:ph - 1, :]
                p1 = prev_gate[ph - 1:ph, :]
            top1 = jnp.where(row8 == 0, p1, r1[:SUBLANES])
            top2 = jnp.where(row8 == 0, p2, jnp.where(row8 == 1, p1, r2[:SUBLANES]))
            g1 = jnp.concatenate([top1, r1[SUBLANES:]], axis=0)
            g2 = jnp.concatenate([top2, r2[SUBLANES:]], axis=0)
        else:
            t = lax.broadcasted_iota(jnp.int32, gate.shape, 0) % tseq
            g1 = jnp.where(t == 0, b1_ref[rows, :], r1)
            g2 = jnp.where(t < 2, b2_ref[rows, :], r2)
            gate_ref[rows, :] = gate
        conv = cb_ref[...] + g2 * cw_ref[0:1, :] + g1 * cw_ref[1:2, :] + gate * cw_ref[2:3, :]
        act_ref[rows, :] = (conv * _sigmoid(conv) * val).astype(BF16)
    if blocks_per_seq >= 1:
        carry_ref[n] = gate[hs - SUBLANES:, :]
        gate_ref[...] = gate[hs - (FFN_CONV - 1):, :]


def _ffn_up(h, g, w_up, conv_w, conv_b, conv_prev, nseq, tseq):
    M, D = h.shape
    Fd = w_up.shape[1] // 2
    tn = _pick_tile(Fd, (256, 128))
    nn = Fd // tn
    if tseq >= 512:
        tm = _pick_tile(tseq, (1024, 512))
        bps = tseq // tm
        b1 = conv_prev[:, 1:2, :]
        b2 = conv_prev[:, 0:1, :]
        bspec = pl.BlockSpec((None, 1, tn), lambda m, n: (m // bps, 0, n))
        gate_shape = jax.ShapeDtypeStruct((M // tm, FFN_CONV - 1, Fd), F32)
        gate_spec = pl.BlockSpec((None, FFN_CONV - 1, tn), lambda m, n: (m, 0, n))
    else:
        tm = M
        bps = 0
        z = jnp.zeros((nseq, tseq, Fd), F32)
        b1 = z.at[:, 0].set(conv_prev[:, 1]).reshape(M, Fd)
        b2 = z.at[:, 0].set(conv_prev[:, 0]).at[:, 1].set(conv_prev[:, 1]).reshape(M, Fd)
        bspec = pl.BlockSpec((tm, tn), lambda m, n: (m, n))
        gate_shape = jax.ShapeDtypeStruct((M, Fd), F32)
        gate_spec = pl.BlockSpec((tm, tn), lambda m, n: (m, n))
    splits = (0, tm // 2, tm)
    act, gate = pl.pallas_call(
        functools.partial(_ffn_up_kernel, tseq=tseq, blocks_per_seq=bps, splits=splits),
        out_shape=[jax.ShapeDtypeStruct((M, Fd), BF16), gate_shape],
        grid=(M // tm, nn),
        in_specs=[
            pl.BlockSpec((tm, D), lambda m, n: (m, 0), pipeline_mode=pl.Buffered(1)),
            pl.BlockSpec((1, D), lambda m, n: (0, 0)),
            pl.BlockSpec((D, tn), lambda m, n: (0, n)),
            pl.BlockSpec((D, tn), lambda m, n: (0, nn + n)),
            pl.BlockSpec((FFN_CONV, tn), lambda m, n: (0, n)),
            pl.BlockSpec((1, tn), lambda m, n: (0, n)),
            bspec, bspec,
        ],
        out_specs=[pl.BlockSpec((tm, tn), lambda m, n: (m, n)), gate_spec],
        scratch_shapes=[pltpu.VMEM((tm, D), BF16), pltpu.VMEM((nn, SUBLANES, tn), F32)],
        compiler_params=_cparams("arbitrary", "arbitrary"),
        name="ffn_up",
    )(h, g.reshape(1, D), w_up, w_up, conv_w, conv_b.reshape(1, Fd), b1, b2)
    if bps == 0:
        gate = gate.reshape(nseq, tseq, Fd)[:, tseq - (FFN_CONV - 1):]
    else:
        gate = gate[bps - 1::bps]
    return act, gate


def _ffn_down_kernel(a_ref, w_ref, h_ref, o_ref):
    o_ref[...] = h_ref[...] + jnp.dot(a_ref[...], w_ref[...], preferred_element_type=F32)


def _ffn_down(act, w_down, h):
    M, Fd = act.shape
    D = w_down.shape[1]
    tm = _pick_tile(M, (512, 256, 128, 64, 32))
    tn = _pick_tile(D, (256, 128))
    return pl.pallas_call(
        _ffn_down_kernel,
        out_shape=jax.ShapeDtypeStruct((M, D), F32),
        grid=(M // tm, D // tn),
        in_specs=[
            pl.BlockSpec((tm, Fd), lambda m, n: (m, 0)),
            pl.BlockSpec((Fd, tn), lambda m, n: (0, n)),
            pl.BlockSpec((tm, tn), lambda m, n: (m, n)),
        ],
        out_specs=pl.BlockSpec((tm, tn), lambda m, n: (m, n)),
        compiler_params=_cparams("parallel", "arbitrary"),
        name="ffn_down",
    )(act, w_down, h)


def _layer_weights(lp):
    (attn_norm_g, w_in, mu, w0, w2, a0, a2, g2, k_k, k_a, r_k, ln_w, ln_b, q_g, k_g, rel_bias,
     w_out, ffn_norm_g, w_up, conv_w, conv_b, w_down) = lp
    D = w_in.shape[0]
    dr = w0.shape[0]
    dl, al, gl = w2.shape[0], a2.shape[0], g2.shape[0]
    nrc = 3 * dr + dl + al + gl
    lpad = _round_up(dl + al + gl, LANES)
    cr = 3 * dr + lpad
    assert cr <= w_in.shape[1]
    w_in_r = w_in[:, :cr].astype(BF16)
    w_in_a = w_in[:, nrc:].astype(BF16)
    mu_p = jnp.concatenate([mu, jnp.zeros((cr - nrc,), mu.dtype)]).reshape(1, cr)
    zrow = lambda n: jnp.zeros((n, dr), F32)
    w2p = jnp.concatenate([w2, zrow(lpad - dl)], axis=0).astype(BF16)
    a2p = jnp.concatenate([zrow(dl), a2, zrow(lpad - dl - al)], axis=0).astype(BF16)
    g2p = jnp.concatenate([zrow(dl + al), g2, zrow(lpad - dl - al - gl)], axis=0).astype(BF16)
    return dict(attn_norm_g=attn_norm_g, w_in_r=w_in_r, w_in_a=w_in_a, mu_p=mu_p, w0=w0, a0=a0, k_k=k_k, k_a=k_a,
                r_k=r_k.reshape(-1), w2p=w2p, a2p=a2p, g2p=g2p, ln_w=ln_w, ln_b=ln_b, q_g=q_g, k_g=k_g,
                rel_bias=rel_bias, w_out=w_out.astype(BF16), ffn_norm_g=ffn_norm_g,
                w_up=w_up.astype(BF16), conv_w=conv_w, conv_b=conv_b, w_down=w_down.astype(BF16),
                nrc=nrc, cr=cr, dr=dr)


def _block(x, shift_prev, wkv_prev, conv_prev, k_cache, v_cache, W, bias_band, bias_dec):
    B, T, D = x.shape
    M = B * T
    dr, cr, nrc = W["dr"], W["cr"], W["nrc"]
    H = (D - dr) // ATT_HEAD
    xf = x.reshape(M, D)
    proj = _rms_matmul(xf, W["attn_norm_g"], W["w_in_r"], W["w_in_a"])

    shift_p = jnp.concatenate([shift_prev, jnp.zeros((B, 1, cr - nrc), F32)], axis=-1)
    prep = _rwkv_prep(proj, shift_p, B, T, cr, dr, W["mu_p"], W["w0"], W["a0"], W["k_k"], W["k_a"],
                      W["r_k"], W["w2p"], W["a2p"], W["g2p"])
    y_r, wkv_new = _wkv(prep, wkv_prev, B, T, dr, W["ln_w"], W["ln_b"])
    shift_new = proj.reshape(B, T, -1)[:, T - 1:, :nrc]

    da = H * ATT_HEAD
    if k_cache is None:
        y_a, k_new = _band_attention(proj, cr, B, T, H, W["q_g"], W["k_g"], bias_band)
        keep = min(ATT_CTX, T)
        k_new = k_new.reshape(B, keep, H, ATT_HEAD)
        v_new = proj.reshape(B, T, -1)[:, T - keep:, cr + 2 * da:].reshape(B, keep, H, ATT_HEAD)
    else:
        L = k_cache.shape[1]
        y_a, k_new = _cached_attention(proj, cr, B, T, H, k_cache.reshape(B, L * H, ATT_HEAD),
                                       v_cache.reshape(B, L * H, ATT_HEAD), W["q_g"], W["k_g"], bias_dec)
        k_new = k_new.reshape(B, T, H, ATT_HEAD)
        v_new = proj.reshape(B, T, -1)[:, :, cr + 2 * da:].reshape(B, T, H, ATT_HEAD)

    h = _outproj(y_r, y_a, W["w_out"], xf)
    act, conv_new = _ffn_up(h, W["ffn_norm_g"], W["w_up"], W["conv_w"], W["conv_b"], conv_prev, B, T)
    y = _ffn_down(act, W["w_down"], h)
    return y.reshape(B, T, D), shift_new, wkv_new, k_new, v_new, conv_new


def kernel(x_prompt, x_sample, state_rwkv_shift, state_rwkv_wkv, cache_attn_k, cache_attn_v,
           state_ffn_conv, attn_norm_g, w_in, rwkv_mu, rwkv_w0, rwkv_w2, rwkv_a0, rwkv_a2, rwkv_g2,
           rwkv_k_k, rwkv_k_a, rwkv_r_k, rwkv_ln_w, rwkv_ln_b, att_q_norm_g, att_k_norm_g,
           att_rel_bias, w_out, ffn_norm_g, ffn_w_up, ffn_conv_w, ffn_conv_b, ffn_w_down):
    depth = w_in.shape[0]
    B = x_prompt.shape[0]
    xp, xs = x_prompt, x_sample
    dec_t = x_sample.shape[1]
    L = cache_attn_k.shape[2]
    assert L == ATT_CTX
    dec_w = _round_up(L + dec_t, LANES)
    p_states = ([], [], [], [], [])
    s_states = ([], [], [], [], [])
    for l in range(depth):
        lp = (attn_norm_g[l], w_in[l], rwkv_mu[l], rwkv_w0[l], rwkv_w2[l], rwkv_a0[l], rwkv_a2[l],
              rwkv_g2[l], rwkv_k_k[l], rwkv_k_a[l], rwkv_r_k[l], rwkv_ln_w[l], rwkv_ln_b[l],
              att_q_norm_g[l], att_k_norm_g[l], att_rel_bias[l], w_out[l], ffn_norm_g[l],
              ffn_w_up[l], ffn_conv_w[l], ffn_conv_b[l], ffn_w_down[l])
        W = _layer_weights(lp)
        nrc, dr = W["nrc"], W["dr"]
        Fd = ffn_conv_b.shape[-1]
        bias_band, bias_dec = _bias_tiles(att_rel_bias[l], ATT_CTX, dec_t, dec_w)
        shift0 = jnp.zeros((B, 1, nrc), F32)
        wkv0 = jnp.zeros((B, dr // RWKV_HEAD, RWKV_HEAD, RWKV_HEAD), F32)
        conv0 = jnp.zeros((B, FFN_CONV - 1, Fd), F32)
        xp, *ps = _block(xp, shift0, wkv0, conv0, None, None, W, bias_band, bias_dec)
        xs, *ss = _block(xs, state_rwkv_shift[l], state_rwkv_wkv[l], state_ffn_conv[l],
                         cache_attn_k[l], cache_attn_v[l], W, bias_band, bias_dec)
        for i in range(5):
            p_states[i].append(ps[i])
            s_states[i].append(ss[i])
    p_out = [jnp.stack(t, axis=0) for t in p_states]
    s_out = [jnp.stack(t, axis=0) for t in s_states]
    return (xp, xs, *p_out, *s_out)
```

```python
import functools
import math

import jax
import jax.numpy as jnp
from jax import lax
from jax.experimental import pallas as pl
from jax.experimental.pallas import tpu as pltpu

F32 = jnp.float32
BF16 = jnp.bfloat16

RMS_EPS = 1e-6
GN_EPS = 64e-5
NEG_INF = -1e30
CHUNK = 64
N_PAST_CHUNKS = 8
ATT_CTX = N_PAST_CHUNKS * CHUNK
REL_CLIP = 128
RWKV_HEAD = 64
ATT_HEAD = 128
FFN_CONV = 3

LANES = 128
SUBLANES = 8
VMEM_LIMIT = 56 * 1024 * 1024
TOEP_W = 2048
RMS_ROW_CHUNK = 256


def _cparams(*sem):
    return pltpu.CompilerParams(dimension_semantics=sem, vmem_limit_bytes=VMEM_LIMIT)


def _round_up(n, m):
    return (n + m - 1) // m * m


def _pick_tile(n, candidates):
    for c in candidates:
        if n % c == 0:
            return c
    raise ValueError(f"no tile in {candidates} divides {n}")


def _pair_ones():
    r = lax.broadcasted_iota(jnp.int32, (LANES, LANES), 0) // RWKV_HEAD
    c = lax.broadcasted_iota(jnp.int32, (LANES, LANES), 1) // RWKV_HEAD
    return (r == c).astype(BF16)


def _seg_sum(x, ones):
    hi = x.astype(BF16)
    lo = (x - hi.astype(F32)).astype(BF16)
    return (jnp.dot(hi, ones, preferred_element_type=F32)
            + jnp.dot(lo, ones, preferred_element_type=F32))


def _seg_sum_wide(x, ones):
    rows, width = x.shape
    blocks = [slice(c * LANES, (c + 1) * LANES) for c in range(width // LANES)]
    r = _seg_sum(jnp.concatenate([x[:, sl] for sl in blocks], axis=0), ones)
    return jnp.concatenate([r[c * rows:(c + 1) * rows] for c in range(len(blocks))], axis=1)


def _sigmoid(z):
    return 1.0 / (1.0 + jnp.exp(-z))


def _rms_rows_to_bf16(x_ref, g_ref, xn_ref, rows):
    rc = min(rows.stop - rows.start, RMS_ROW_CHUNK)
    parts = []
    for r0 in range(rows.start, rows.stop, rc):
        x = x_ref[r0:r0 + rc, :]
        ms = jnp.mean(x * x, axis=-1, keepdims=True)
        xn = (x * lax.rsqrt(ms + RMS_EPS) * g_ref[...]).astype(BF16)
        xn_ref[r0:r0 + rc, :] = xn
        parts.append(xn)
    return parts[0] if len(parts) == 1 else jnp.concatenate(parts, axis=0)


def _rms_matmul_kernel(x_ref, g_ref, wa_ref, wb_ref, o_ref, xn_ref, *, na):
    n = pl.program_id(1)

    @pl.when(n == 0)
    def _():
        tm = x_ref.shape[0]
        rc = min(tm, RMS_ROW_CHUNK)
        for r0 in range(0, tm, rc):
            rows = slice(r0, r0 + rc)
            xn = _rms_rows_to_bf16(x_ref, g_ref, xn_ref, rows)
            o_ref[rows, :] = jnp.dot(xn, wa_ref[...], preferred_element_type=F32)

    @pl.when((n > 0) & (n < na))
    def _():
        o_ref[...] = jnp.dot(xn_ref[...], wa_ref[...], preferred_element_type=F32)

    @pl.when(n >= na)
    def _():
        o_ref[...] = jnp.dot(xn_ref[...], wb_ref[...], preferred_element_type=F32)


def _rms_matmul(x, g, wa, wb):
    M, D = x.shape
    Na, Nb = wa.shape[1], wb.shape[1]
    tm = _pick_tile(M, (1024, 512, 256, 128, 64, 32))
    tn = _pick_tile(math.gcd(Na, Nb), (512, 256, 128))
    na, nb = Na // tn, Nb // tn
    return pl.pallas_call(
        functools.partial(_rms_matmul_kernel, na=na),
        out_shape=jax.ShapeDtypeStruct((M, Na + Nb), F32),
        grid=(M // tm, na + nb),
        in_specs=[
            pl.BlockSpec((tm, D), lambda m, n: (m, 0), pipeline_mode=pl.Buffered(1)),
            pl.BlockSpec((1, D), lambda m, n: (0, 0)),
            pl.BlockSpec((D, tn), lambda m, n: (0, jnp.minimum(n, na - 1))),
            pl.BlockSpec((D, tn), lambda m, n: (0, jnp.maximum(n - na, 0))),
        ],
        out_specs=pl.BlockSpec((tm, tn), lambda m, n: (m, n)),
        scratch_shapes=[pltpu.VMEM((tm, D), BF16)],
        compiler_params=_cparams("parallel", "arbitrary"),
        name="rms_inproj",
    )(x, g.reshape(1, D), wa, wb)


def _rwkv_prep_kernel(p_ref, prev_ref, mu_ref, w0_ref, a0_ref, kk_ref, ka_ref, rk_ref,
                      w2_ref, a2_ref, g2_ref,
                      a_out, b_out, w_out, k_out, rp_out, v_out, g_out, kr_out, bon_out,
                      last_ref, *, dr):
    @pl.when(pl.program_id(1) == 0)
    def _():
        last_ref[...] = prev_ref[...]

    p = p_ref[...]
    tb = p.shape[0]
    row = lax.broadcasted_iota(jnp.int32, p.shape, 0)
    shifted = jnp.where(row == 0, last_ref[...], pltpu.roll(p, 1, axis=0))
    last_ref[...] = p[tb - 1:tb, :]
    xm = p + (shifted - p) * mu_ref[...]

    lo = xm[:, 3 * dr:]
    wl = jnp.dot(jnp.tanh(lo).astype(BF16), w2_ref[...], preferred_element_type=F32)
    al = jnp.dot(lo.astype(BF16), a2_ref[...], preferred_element_type=F32)
    g_out[...] = jnp.dot(_sigmoid(lo).astype(BF16), g2_ref[...], preferred_element_type=F32)

    ones = _pair_ones()
    r = xm[:, :dr]
    k = xm[:, dr:2 * dr]
    v_out[...] = xm[:, 2 * dr:3 * dr]
    decay = jnp.exp(-math.exp(-0.5) * _sigmoid(w0_ref[...] + wl))
    asig = _sigmoid(a0_ref[...] + al)
    kk = k * kk_ref[...]
    kk = kk * lax.rsqrt(jnp.maximum(_seg_sum_wide(kk * kk, ones), 1e-24))
    k2 = k * (1.0 + (asig - 1.0) * ka_ref[...])
    b_s = kk * asig
    rk2 = r * k2
    a_out[...] = -kk
    b_out[...] = b_s
    w_out[...] = decay
    k_out[...] = k2
    rp_out[...] = decay * r - kk * _seg_sum_wide(b_s * r, ones)
    kr_out[...] = _seg_sum_wide(rk2, ones)
    bon_out[...] = _seg_sum_wide(rk2 * rk_ref[...], ones)


def _rwkv_prep(proj, shift_prev, nseq, tseq, cr, dr, mu_p, w0, a0, k_k, k_a, r_k, w2p, a2p, g2p):
    M = proj.shape[0]
    tb = _pick_tile(tseq, (128, 64, 32))
    nt = tseq // tb
    lp = cr - 3 * dr
    row = lambda a: a.reshape(1, -1).astype(F32)
    vec_spec = pl.BlockSpec((1, dr), lambda b, i: (0, 0))
    mat_spec = pl.BlockSpec((lp, dr), lambda b, i: (0, 0))
    out_spec = pl.BlockSpec((tb, dr), lambda b, i: (b * nt + i, 0))
    outs = pl.pallas_call(
        functools.partial(_rwkv_prep_kernel, dr=dr),
        out_shape=[jax.ShapeDtypeStruct((M, dr), F32)] * 9,
        grid=(nseq, nt),
        in_specs=[
            pl.BlockSpec((tb, cr), lambda b, i: (b * nt + i, 0)),
            pl.BlockSpec((None, 1, cr), lambda b, i: (b, 0, 0)),
            pl.BlockSpec((1, cr), lambda b, i: (0, 0)),
            vec_spec, vec_spec, vec_spec, vec_spec, vec_spec,
            mat_spec, mat_spec, mat_spec,
        ],
        out_specs=[out_spec] * 9,
        scratch_shapes=[pltpu.VMEM((1, cr), F32)],
        compiler_params=_cparams("parallel", "arbitrary"),
        name="rwkv_prep",
    )(proj, shift_prev, mu_p, row(w0), row(a0), row(k_k), row(k_a), row(r_k), w2p, a2p, g2p)
    return outs


def _pair_transpose(x):
    t2 = jnp.concatenate([x, x], axis=0).T
    lane = lax.broadcasted_iota(jnp.int32, x.shape, 1)
    return jnp.where(lane < RWKV_HEAD, t2[:RWKV_HEAD], t2[RWKV_HEAD:])


def _stack_heads(x):
    lane = lax.broadcasted_iota(jnp.int32, x.shape, 1)
    return jnp.concatenate([jnp.where(lane < RWKV_HEAD, x, 0.0),
                            jnp.where(lane >= RWKV_HEAD, x, 0.0)], axis=0)


def _wkv_kernel(a_ref, b_ref, w_ref, k_ref, rp_ref, v_ref, g_ref, kr_ref, bon_ref,
                s0_ref, lnw_ref, lnb_ref, y_ref, s_out_ref,
                s_scr, vt_scr, ut_scr, y0_scr, vd_scr, cf_scr, df_scr, kf_scr,
                at_scr, bt_scr, cw_scr, *, tb, npairs, nb):
    i = pl.program_id(1)
    n = RWKV_HEAD

    units = [(bi, p) for bi in range(nb) for p in range(npairs)]
    nu = len(units)
    col = lambda p: slice(p * LANES, (p + 1) * LANES)

    @pl.when(i == 0)
    def _():
        for q, (bi, p) in enumerate(units):
            s_scr[q] = jnp.concatenate([s0_ref[bi, 2 * p], s0_ref[bi, 2 * p + 1]], axis=1)

    ones = _pair_ones()
    s_lt_j = (lax.broadcasted_iota(jnp.int32, (LANES, LANES), 0) % n
              < lax.broadcasted_iota(jnp.int32, (LANES, LANES), 1) % n)

    def pad_rows(x):
        return x if tb == n else jnp.concatenate([x, jnp.zeros((n - tb, LANES), F32)], axis=0)

    trow_all = lax.broadcasted_iota(jnp.int32, w_ref.shape[1:], 0)
    kt_all, rho_all = [], []
    for bi in range(nb):
        cw = w_ref[bi]
        shift = 1
        while shift < tb:
            cw = jnp.where(trow_all >= shift, cw * pltpu.roll(cw, shift, axis=0), cw)
            shift *= 2
        cw_prev = jnp.where(trow_all == 0, 1.0, pltpu.roll(cw, 1, axis=0))
        inv = 1.0 / cw
        at_scr[bi] = a_ref[bi] * cw_prev
        bt_scr[bi] = b_ref[bi] * inv
        cw_scr[bi] = cw[tb - 1:tb]
        kt_all.append(k_ref[bi] * inv)
        rho_all.append(rp_ref[bi] * cw_prev)

    stacked = lambda x: _stack_heads(pad_rows(x)).astype(BF16)
    rhos = [stacked(rho_all[bi][:, col(p)]) for bi, p in units]
    afs = [stacked(at_scr[bi, :, col(p)]) for bi, p in units]
    bfs = [stacked(bt_scr[bi, :, col(p)]) for bi, p in units]
    kfs = [stacked(kt_all[bi][:, col(p)]) for bi, p in units]
    r1s = [_dot_nt(jnp.concatenate([bfs[q], kfs[q], s_scr[q].astype(BF16)], axis=0), rhos[q])
           for q in range(nu)]
    kas = [_dot_nt(kfs[q], afs[q]) for q in range(nu)]
    vts = [_pair_transpose(pad_rows(v_ref[bi, :, col(p)])).astype(BF16) for bi, p in units]
    for q in range(nu):
        vt_scr[q] = vts[q]
        kf_scr[q] = kfs[q]
        ut_scr[q] = jnp.zeros((n, LANES), F32)
        cf_scr[q] = jnp.where(s_lt_j, r1s[q][:LANES], 0.0).astype(BF16)
        df_scr[q] = jnp.where(s_lt_j, r1s[q][LANES:2 * LANES], 0.0).astype(BF16)
        y0_scr[q] = r1s[q][2 * LANES:]
    kas = [jnp.where(s_lt_j, ka, 0.0).astype(BF16) for ka in kas]
    for q in range(nu):
        vd_scr[q] = jnp.dot(vts[q], kas[q], preferred_element_type=F32).astype(BF16)

    lane = lax.broadcasted_iota(jnp.int32, (n, LANES), 1) % n

    gsize = 8 if nu % 8 == 0 else nu
    ngroups = nu // gsize

    def steps(t8, carry):
        rows = pl.ds(pl.multiple_of(t8 * SUBLANES, SUBLANES), SUBLANES)
        for j in range(SUBLANES):
            mask = lane == t8 * SUBLANES + j
            row = lambda ref, q: ref[units[q][0], rows, col(units[q][1])][j:j + 1]
            for g in range(ngroups):
                qs = range(g * gsize, (g + 1) * gsize)
                lhs = jnp.concatenate(
                    [(s_scr[q] * row(at_scr, q)).astype(BF16)
                     + jnp.where(mask, vd_scr[q], jnp.zeros((), BF16)) for q in qs], axis=0)
                res = jnp.dot(lhs, ones, preferred_element_type=F32)
                for x, q in enumerate(qs):
                    u = res[x * n:(x + 1) * n]
                    s_scr[q] = s_scr[q] + u * row(bt_scr, q)
                    pltpu.store(ut_scr.at[q], u, mask=mask)
        return carry

    lax.fori_loop(0, tb // SUBLANES, steps, 0)

    inv_n = 1.0 / n

    def transposed_outputs(bi):
        ys = []
        for p in range(npairs):
            q = bi * npairs + p
            vt = vt_scr[q]
            z_end = s_scr[q] + jnp.dot(vt, kf_scr[q], preferred_element_type=F32)
            s_scr[q] = z_end * cw_scr[bi, :, col(p)]
            yt = (y0_scr[q]
                  + jnp.dot(ut_scr[q].astype(BF16), cf_scr[q], preferred_element_type=F32)
                  + jnp.dot(vt, df_scr[q], preferred_element_type=F32))
            ys.append(_pair_transpose(yt)[:tb])
        return jnp.concatenate(ys, axis=0)

    def normalise_and_gate(bi, y):
        stack = lambda ref: jnp.concatenate([ref[bi, :, col(p)] for p in range(npairs)], axis=0)
        v_all = stack(v_ref)
        y = y + v_all * stack(kr_ref)
        mean = _seg_sum(y, ones) * inv_n
        d = y - mean
        var = _seg_sum(d * d, ones) * inv_n
        yn = d * lax.rsqrt(var + GN_EPS)
        bonus = stack(bon_ref) * v_all
        for p in range(npairs):
            rows_p = slice(p * tb, (p + 1) * tb)
            out = (yn[rows_p] * lnw_ref[:, col(p)] + lnb_ref[:, col(p)] + bonus[rows_p]) * g_ref[bi, :, col(p)]
            y_ref[bi, :, col(p)] = out.astype(BF16)

    y_prev = None
    for bi in range(nb):
        y_cur = transposed_outputs(bi)
        if y_prev is not None:
            normalise_and_gate(bi - 1, y_prev)
        y_prev = y_cur
    normalise_and_gate(nb - 1, y_prev)

    @pl.when(i == pl.num_programs(1) - 1)
    def _():
        for q, (bi, p) in enumerate(units):
            s = s_scr[q]
            s_out_ref[bi, 2 * p] = s[:, :n]
            s_out_ref[bi, 2 * p + 1] = s[:, n:]


def _wkv(prep, s0, nseq, tseq, dr, ln_w, ln_b):
    M = prep[0].shape[0]
    tb = min(tseq, RWKV_HEAD)
    nt = tseq // tb
    npairs = dr // LANES
    nb = 2
    assert nseq % nb == 0
    nu = nb * npairs
    prep = [a.reshape(nseq, tseq, dr) for a in prep]
    blk = pl.BlockSpec((nb, tb, dr), lambda b, i: (b, i, 0))
    st = pl.BlockSpec((nb, 2 * npairs, RWKV_HEAD, RWKV_HEAD), lambda b, i: (b, 0, 0, 0))
    vec = pl.BlockSpec((1, dr), lambda b, i: (0, 0))
    y, s_new = pl.pallas_call(
        functools.partial(_wkv_kernel, tb=tb, npairs=npairs, nb=nb),
        out_shape=[jax.ShapeDtypeStruct((nseq, tseq, dr), BF16),
                   jax.ShapeDtypeStruct(s0.shape, F32)],
        grid=(nseq // nb, nt),
        in_specs=[blk] * 9 + [st, vec, vec],
        out_specs=[blk, st],
        scratch_shapes=[pltpu.VMEM((nu, RWKV_HEAD, LANES), F32),
                        pltpu.VMEM((nu, RWKV_HEAD, LANES), BF16),
                        pltpu.VMEM((nu, RWKV_HEAD, LANES), F32),
                        pltpu.VMEM((nu, RWKV_HEAD, LANES), F32),
                        pltpu.VMEM((nu, RWKV_HEAD, LANES), BF16),
                        pltpu.VMEM((nu, LANES, LANES), BF16),
                        pltpu.VMEM((nu, LANES, LANES), BF16),
                        pltpu.VMEM((nu, LANES, LANES), BF16),
                        pltpu.VMEM((nb, tb, dr), F32), pltpu.VMEM((nb, tb, dr), F32),
                        pltpu.VMEM((nb, 1, dr), F32)],
        compiler_params=_cparams("parallel", "arbitrary"),
        name="wkv7",
    )(*prep, s0, ln_w.reshape(1, dr).astype(F32), ln_b.reshape(1, dr).astype(F32))
    return y.reshape(M, dr), s_new


def _bias_kernel(tab_ref, band_ref, dec_ref, *, nrel, qb, dec_t, dec_w):
    h = pl.program_id(0)
    m = lax.broadcasted_iota(jnp.int32, (SUBLANES, TOEP_W), 1)
    dd = jnp.where(m < TOEP_W // 2, m, m - TOEP_W)
    idx = jnp.clip(dd - ATT_CTX, -REL_CLIP, REL_CLIP) + REL_CLIP

    def pick(r, acc):
        return jnp.where(idx == r, tab_ref[h * nrel + r], acc)

    e = lax.fori_loop(0, nrel, pick, jnp.zeros((SUBLANES, TOEP_W), F32))
    eb = jnp.broadcast_to(e[0:1], (qb, TOEP_W))
    toe = pltpu.roll(eb, 0, axis=1, stride=1, stride_axis=0)
    kb = band_ref.shape[-1]
    cq = lax.broadcasted_iota(jnp.int32, (qb, kb), 0) // CHUNK
    ck = lax.broadcasted_iota(jnp.int32, (qb, kb), 1) // CHUNK
    valid = (ck >= cq) & (ck <= cq + N_PAST_CHUNKS)
    band_ref[...] = jnp.where(valid, toe[:, :kb], NEG_INF)
    dec_ref[...] = toe[:dec_t, :dec_w]


def _bias_tiles(table, qb, dec_t, dec_w):
    nrel, H = table.shape
    tab = table.T.reshape(-1).astype(F32)
    return pl.pallas_call(
        functools.partial(_bias_kernel, nrel=nrel, qb=qb, dec_t=dec_t, dec_w=dec_w),
        out_shape=[jax.ShapeDtypeStruct((H, qb, qb + ATT_CTX), F32),
                   jax.ShapeDtypeStruct((H, dec_t, dec_w), F32)],
        grid=(H,),
        in_specs=[pl.BlockSpec(memory_space=pltpu.SMEM)],
        out_specs=[pl.BlockSpec((None, qb, qb + ATT_CTX), lambda h: (h, 0, 0)),
                   pl.BlockSpec((None, dec_t, dec_w), lambda h: (h, 0, 0))],
        compiler_params=_cparams("arbitrary"),
        name="rel_bias_tiles",
    )(tab)


def _head_rms(x, g):
    return x * lax.rsqrt(jnp.mean(x * x, axis=-1, keepdims=True) + RMS_EPS) * g


def _dot_nt(a, b):
    return lax.dot_general(a, b, (((1,), (1,)), ((), ())), preferred_element_type=F32)


def _heads_per_step(nheads, col0, most):
    for hps in (4, 2, 1):
        if hps <= most and nheads % hps == 0 and col0 % (hps * ATT_HEAD) == 0:
            return hps
    raise ValueError("attention columns are not head aligned")


def _band_attn_kernel(q_ref, kc_ref, kp_ref, vc_ref, vp_ref, qg_ref, kg_ref, bias_ref,
                      o_ref, knew_ref, *, hps):
    i = pl.program_id(2)
    scale = ATT_HEAD ** -0.5
    qb = q_ref.shape[0]
    qs = 2 * CHUNK
    ks = qs + ATT_CTX
    kidx = lax.broadcasted_iota(jnp.int32, (qs, ks), 1)
    heads = [slice(hh * ATT_HEAD, (hh + 1) * ATT_HEAD) for hh in range(hps)]
    qn = [(_head_rms(q_ref[:, sl], qg_ref[...]) * scale).astype(BF16) for sl in heads]
    kc = [_head_rms(kc_ref[:, sl], kg_ref[...]) for sl in heads]
    kp = [_head_rms(kp_ref[:, sl], kg_ref[...]) for sl in heads]
    k_all = [jnp.concatenate([kp[hh], kc[hh]], axis=0).astype(BF16) for hh in range(hps)]
    v_all = [jnp.concatenate([vp_ref[:, sl], vc_ref[:, sl]], axis=0).astype(BF16) for sl in heads]
    items = [(hh, r * qs) for hh in range(hps) for r in range(qb // qs)]
    s = [_dot_nt(qn[hh][r0:r0 + qs], k_all[hh][r0:r0 + ks]) + bias_ref[hh, r0:r0 + qs, r0:r0 + ks]
         for hh, r0 in items]
    s = [jnp.where((i > 0) | (kidx >= ATT_CTX - r0), x, NEG_INF) for x, (hh, r0) in zip(s, items)]
    m = [jnp.max(x, axis=-1, keepdims=True) for x in s]
    p = [jnp.exp(x - y) for x, y in zip(s, m)]
    l = [jnp.sum(x, axis=-1, keepdims=True) for x in p]
    o = [jnp.dot(x.astype(BF16), v_all[hh][r0:r0 + ks], preferred_element_type=F32)
         for x, (hh, r0) in zip(p, items)]
    for x, y, (hh, r0) in zip(o, l, items):
        o_ref[r0:r0 + qs, heads[hh]] = (x / y).astype(BF16)

    @pl.when(i == pl.num_programs(2) - 1)
    def _():
        for hh in range(hps):
            knew_ref[:, heads[hh]] = kc[hh]


def _band_attention(proj, col0, nseq, tseq, nheads, q_g, k_g, bias_band):
    M = proj.shape[0]
    qb = ATT_CTX
    assert tseq % qb == 0
    nb = tseq // qb
    H = nheads
    hps = _heads_per_step(H, col0, 4)
    w = hps * ATT_HEAD
    c0 = col0 // w
    hb = H // hps
    cur = lambda off: pl.BlockSpec((qb, w), lambda b, h, i: (b * nb + i, c0 + off + h))
    prv = lambda off: pl.BlockSpec(
        (qb, w), lambda b, h, i: (b * nb + jnp.maximum(i - 1, 0), c0 + off + h))
    gspec = pl.BlockSpec((1, ATT_HEAD), lambda b, h, i: (0, 0))
    return pl.pallas_call(
        functools.partial(_band_attn_kernel, hps=hps),
        out_shape=[jax.ShapeDtypeStruct((M, H * ATT_HEAD), BF16),
                   jax.ShapeDtypeStruct((nseq, qb, H * ATT_HEAD), F32)],
        grid=(nseq, hb, nb),
        in_specs=[cur(0), cur(hb), prv(hb), cur(2 * hb), prv(2 * hb), gspec, gspec,
                  pl.BlockSpec((hps, qb, qb + ATT_CTX), lambda b, h, i: (h, 0, 0))],
        out_specs=[pl.BlockSpec((qb, w), lambda b, h, i: (b * nb + i, h)),
                   pl.BlockSpec((None, qb, w), lambda b, h, i: (b, 0, h))],
        compiler_params=_cparams("parallel", "parallel", "arbitrary"),
        name="band_attention",
    )(proj, proj, proj, proj, proj, q_g.reshape(1, -1), k_g.reshape(1, -1), bias_band)


def _cached_attn_kernel(q_ref, kn_ref, vn_ref, kc_ref, vc_ref, qg_ref, kg_ref, bias_ref,
                        o_ref, knew_ref, *, t, hps, nheads):
    scale = ATT_HEAD ** -0.5
    L = kc_ref.shape[0] // nheads
    h0 = pl.program_id(1) * hps
    heads = [slice(h * ATT_HEAD, (h + 1) * ATT_HEAD) for h in range(hps)]
    rows = [pl.ds(h0 + h, L, stride=nheads) for h in range(hps)]
    qn = [(_head_rms(q_ref[:, sl], qg_ref[...]) * scale).astype(BF16) for sl in heads]
    kn = [_head_rms(kn_ref[:, sl], kg_ref[...]) for sl in heads]
    s_c = [_dot_nt(qn[h], kc_ref[rows[h], :].astype(BF16)) + bias_ref[h, :, :L] for h in range(hps)]
    s_n = [_dot_nt(qn[h], kn[h].astype(BF16)) + bias_ref[h, :, L:L + t] for h in range(hps)]
    m = [jnp.maximum(jnp.max(a, axis=-1, keepdims=True), jnp.max(b, axis=-1, keepdims=True))
         for a, b in zip(s_c, s_n)]
    p_c = [jnp.exp(a - y) for a, y in zip(s_c, m)]
    p_n = [jnp.exp(b - y) for b, y in zip(s_n, m)]
    l = [jnp.sum(a, axis=-1, keepdims=True) + jnp.sum(b, axis=-1, keepdims=True)
         for a, b in zip(p_c, p_n)]
    o = [jnp.dot(p_c[h].astype(BF16), vc_ref[rows[h], :].astype(BF16), preferred_element_type=F32)
         + jnp.dot(p_n[h].astype(BF16), vn_ref[:, heads[h]].astype(BF16), preferred_element_type=F32)
         for h in range(hps)]
    for h in range(hps):
        knew_ref[:, heads[h]] = kn[h]
        o_ref[:, heads[h]] = (o[h] / l[h]).astype(BF16)


def _cached_attention(proj, col0, nseq, tseq, nheads, k_cache, v_cache, q_g, k_g, bias_dec):
    M = proj.shape[0]
    H = nheads
    L = k_cache.shape[1] // H
    da = H * ATT_HEAD
    hps = _heads_per_step(H, col0, 4)
    w = hps * ATT_HEAD
    c0 = col0 // w
    hb = H // hps
    new = lambda off: pl.BlockSpec((tseq, w), lambda b, h: (b, c0 + off + h))
    cache = pl.BlockSpec((None, L * H, ATT_HEAD), lambda b, h: (b, 0, 0))
    gspec = pl.BlockSpec((1, ATT_HEAD), lambda b, h: (0, 0))
    out = pl.BlockSpec((tseq, w), lambda b, h: (b, h))
    return pl.pallas_call(
        functools.partial(_cached_attn_kernel, t=tseq, hps=hps, nheads=H),
        out_shape=[jax.ShapeDtypeStruct((M, da), BF16),
                   jax.ShapeDtypeStruct((M, da), F32)],
        grid=(nseq, hb),
        in_specs=[new(0), new(hb), new(2 * hb), cache, cache, gspec, gspec,
                  pl.BlockSpec((hps,) + bias_dec.shape[1:], lambda b, h: (h, 0, 0))],
        out_specs=[out, out],
        compiler_params=_cparams("parallel", "arbitrary"),
        name="cached_attention",
    )(proj, proj, proj, k_cache, v_cache, q_g.reshape(1, -1), k_g.reshape(1, -1), bias_dec)


def _outproj_kernel(yr_ref, ya_ref, w_ref, x_ref, h_ref, *, dr):
    acc = jnp.dot(yr_ref[...], w_ref[:dr, :], preferred_element_type=F32)
    acc += jnp.dot(ya_ref[...], w_ref[dr:, :], preferred_element_type=F32)
    h_ref[...] = x_ref[...] + acc


def _outproj(y_r, y_a, w_out, x):
    M, D = x.shape
    dr, da = y_r.shape[1], y_a.shape[1]
    tm = _pick_tile(M, (1024, 512, 256, 128, 64, 32))
    tn = _pick_tile(D, (512, 256, 128))
    return pl.pallas_call(
        functools.partial(_outproj_kernel, dr=dr),
        out_shape=jax.ShapeDtypeStruct((M, D), F32),
        grid=(M // tm, D // tn),
        in_specs=[
            pl.BlockSpec((tm, dr), lambda m, n: (m, 0)),
            pl.BlockSpec((tm, da), lambda m, n: (m, 0)),
            pl.BlockSpec((dr + da, tn), lambda m, n: (0, n)),
            pl.BlockSpec((tm, tn), lambda m, n: (m, n)),
        ],
        out_specs=pl.BlockSpec((tm, tn), lambda m, n: (m, n)),
        compiler_params=_cparams("parallel", "arbitrary"),
        name="out_proj",
    )(y_r, y_a, w_out, x)


def _ffn_up_kernel(h_ref, g_ref, wg_ref, wv_ref, cw_ref, cb_ref, b1_ref, b2_ref,
                   act_ref, gate_ref, hn_ref, carry_ref, *, tseq, blocks_per_seq, splits):
    m = pl.program_id(0)
    n = pl.program_id(1)

    @pl.when(n == 0)
    def _():
        _ffn_up_block(m, n, True, h_ref, g_ref, wg_ref, wv_ref, cw_ref, cb_ref, b1_ref, b2_ref,
                      act_ref, gate_ref, hn_ref, carry_ref, tseq, blocks_per_seq, splits)

    @pl.when(n > 0)
    def _():
        _ffn_up_block(m, n, False, h_ref, g_ref, wg_ref, wv_ref, cw_ref, cb_ref, b1_ref, b2_ref,
                      act_ref, gate_ref, hn_ref, carry_ref, tseq, blocks_per_seq, splits)


def _ffn_up_block(m, n, first_col, h_ref, g_ref, wg_ref, wv_ref, cw_ref, cb_ref, b1_ref, b2_ref,
                  act_ref, gate_ref, hn_ref, carry_ref, tseq, blocks_per_seq, splits):
    tn = wg_ref.shape[1]
    row8 = lax.broadcasted_iota(jnp.int32, (SUBLANES, tn), 0)
    gate = None
    for c in range(len(splits) - 1):
        rows = slice(splits[c], splits[c + 1])
        hs = splits[c + 1] - splits[c]
        hn = _rms_rows_to_bf16(h_ref, g_ref, hn_ref, rows) if first_col else hn_ref[rows, :]
        prev_gate = gate
        gate = jnp.dot(hn, wg_ref[...], preferred_element_type=F32)
        val = jnp.dot(hn, wv_ref[...], preferred_element_type=F32)
        r1 = pltpu.roll(gate, 1, axis=0)
        r2 = pltpu.roll(gate, 2, axis=0)
        if blocks_per_seq >= 1:
            if c == 0:
                first = m % blocks_per_seq == 0
                prev = carry_ref[n]
                p2 = jnp.where(first, b2_ref[0:1, :], prev[SUBLANES - 2:SUBLANES - 1, :])
                p1 = jnp.where(first, b1_ref[0:1, :], prev[SUBLANES - 1:SUBLANES, :])
            else:
                ph = prev_gate.shape[0]
                p2 = prev_gate[ph - 2:ph - 1, :]
                p1 = prev_gate[ph - 1:ph, :]
            top1 = jnp.where(row8 == 0, p1, r1[:SUBLANES])
            top2 = jnp.where(row8 == 0, p2, jnp.where(row8 == 1, p1, r2[:SUBLANES]))
            g1 = jnp.concatenate([top1, r1[SUBLANES:]], axis=0)
            g2 = jnp.concatenate([top2, r2[SUBLANES:]], axis=0)
        else:
            t = lax.broadcasted_iota(jnp.int32, gate.shape, 0) % tseq
            g1 = jnp.where(t == 0, b1_ref[rows, :], r1)
            g2 = jnp.where(t < 2, b2_ref[rows, :], r2)
            gate_ref[rows, :] = gate
        conv = cb_ref[...] + g2 * cw_ref[0:1, :] + g1 * cw_ref[1:2, :] + gate * cw_ref[2:3, :]
        act_ref[rows, :] = (conv * _sigmoid(conv) * val).astype(BF16)
    if blocks_per_seq >= 1:
        carry_ref[n] = gate[hs - SUBLANES:, :]
        gate_ref[...] = gate[hs - (FFN_CONV - 1):, :]


def _ffn_up(h, g, w_up, conv_w, conv_b, conv_prev, nseq, tseq):
    M, D = h.shape
    Fd = w_up.shape[1] // 2
    tn = _pick_tile(Fd, (256, 128))
    nn = Fd // tn
    if tseq >= 512:
        tm = _pick_tile(tseq, (1024, 512))
        bps = tseq // tm
        b1 = conv_prev[:, 1:2, :]
        b2 = conv_prev[:, 0:1, :]
        bspec = pl.BlockSpec((None, 1, tn), lambda m, n: (m // bps, 0, n))
        gate_shape = jax.ShapeDtypeStruct((M // tm, FFN_CONV - 1, Fd), F32)
        gate_spec = pl.BlockSpec((None, FFN_CONV - 1, tn), lambda m, n: (m, 0, n))
    else:
        tm = M
        bps = 0
        z = jnp.zeros((nseq, tseq, Fd), F32)
        b1 = z.at[:, 0].set(conv_prev[:, 1]).reshape(M, Fd)
        b2 = z.at[:, 0].set(conv_prev[:, 0]).at[:, 1].set(conv_prev[:, 1]).reshape(M, Fd)
        bspec = pl.BlockSpec((tm, tn), lambda m, n: (m, n))
        gate_shape = jax.ShapeDtypeStruct((M, Fd), F32)
        gate_spec = pl.BlockSpec((tm, tn), lambda m, n: (m, n))
    splits = (0, tm // 2, tm)
    act, gate = pl.pallas_call(
        functools.partial(_ffn_up_kernel, tseq=tseq, blocks_per_seq=bps, splits=splits),
        out_shape=[jax.ShapeDtypeStruct((M, Fd), BF16), gate_shape],
        grid=(M // tm, nn),
        in_specs=[
            pl.BlockSpec((tm, D), lambda m, n: (m, 0), pipeline_mode=pl.Buffered(1)),
            pl.BlockSpec((1, D), lambda m, n: (0, 0)),
            pl.BlockSpec((D, tn), lambda m, n: (0, n)),
            pl.BlockSpec((D, tn), lambda m, n: (0, nn + n)),
            pl.BlockSpec((FFN_CONV, tn), lambda m, n: (0, n)),
            pl.BlockSpec((1, tn), lambda m, n: (0, n)),
            bspec, bspec,
        ],
        out_specs=[pl.BlockSpec((tm, tn), lambda m, n: (m, n)), gate_spec],
        scratch_shapes=[pltpu.VMEM((tm, D), BF16), pltpu.VMEM((nn, SUBLANES, tn), F32)],
        compiler_params=_cparams("arbitrary", "arbitrary"),
        name="ffn_up",
    )(h, g.reshape(1, D), w_up, w_up, conv_w, conv_b.reshape(1, Fd), b1, b2)
    if bps == 0:
        gate = gate.reshape(nseq, tseq, Fd)[:, tseq - (FFN_CONV - 1):]
    else:
        gate = gate[bps - 1::bps]
    return act, gate


def _ffn_down_kernel(a_ref, w_ref, h_ref, o_ref):
    o_ref[...] = h_ref[...] + jnp.dot(a_ref[...], w_ref[...], preferred_element_type=F32)


def _ffn_down(act, w_down, h):
    M, Fd = act.shape
    D = w_down.shape[1]
    tm = _pick_tile(M, (512, 256, 128, 64, 32))
    tn = _pick_tile(D, (256, 128))
    return pl.pallas_call(
        _ffn_down_kernel,
        out_shape=jax.ShapeDtypeStruct((M, D), F32),
        grid=(M // tm, D // tn),
        in_specs=[
            pl.BlockSpec((tm, Fd), lambda m, n: (m, 0)),
            pl.BlockSpec((Fd, tn), lambda m, n: (0, n)),
            pl.BlockSpec((tm, tn), lambda m, n: (m, n)),
        ],
        out_specs=pl.BlockSpec((tm, tn), lambda m, n: (m, n)),
        compiler_params=_cparams("parallel", "arbitrary"),
        name="ffn_down",
    )(act, w_down, h)


def _layer_weights(lp):
    (attn_norm_g, w_in, mu, w0, w2, a0, a2, g2, k_k, k_a, r_k, ln_w, ln_b, q_g, k_g, rel_bias,
     w_out, ffn_norm_g, w_up, conv_w, conv_b, w_down) = lp
    D = w_in.shape[0]
    dr = w0.shape[0]
    dl, al, gl = w2.shape[0], a2.shape[0], g2.shape[0]
    nrc = 3 * dr + dl + al + gl
    lpad = _round_up(dl + al + gl, LANES)
    cr = 3 * dr + lpad
    assert cr <= w_in.shape[1]
    w_in_r = w_in[:, :cr].astype(BF16)
    w_in_a = w_in[:, nrc:].astype(BF16)
    mu_p = jnp.concatenate([mu, jnp.zeros((cr - nrc,), mu.dtype)]).reshape(1, cr)
    zrow = lambda n: jnp.zeros((n, dr), F32)
    w2p = jnp.concatenate([w2, zrow(lpad - dl)], axis=0).astype(BF16)
    a2p = jnp.concatenate([zrow(dl), a2, zrow(lpad - dl - al)], axis=0).astype(BF16)
    g2p = jnp.concatenate([zrow(dl + al), g2, zrow(lpad - dl - al - gl)], axis=0).astype(BF16)
    return dict(attn_norm_g=attn_norm_g, w_in_r=w_in_r, w_in_a=w_in_a, mu_p=mu_p, w0=w0, a0=a0, k_k=k_k, k_a=k_a,
                r_k=r_k.reshape(-1), w2p=w2p, a2p=a2p, g2p=g2p, ln_w=ln_w, ln_b=ln_b, q_g=q_g, k_g=k_g,
                rel_bias=rel_bias, w_out=w_out.astype(BF16), ffn_norm_g=ffn_norm_g,
                w_up=w_up.astype(BF16), conv_w=conv_w, conv_b=conv_b, w_down=w_down.astype(BF16),
                nrc=nrc, cr=cr, dr=dr)


def _block(x, shift_prev, wkv_prev, conv_prev, k_cache, v_cache, W, bias_band, bias_dec):
    B, T, D = x.shape
    M = B * T
    dr, cr, nrc = W["dr"], W["cr"], W["nrc"]
    H = (D - dr) // ATT_HEAD
    xf = x.reshape(M, D)
    proj = _rms_matmul(xf, W["attn_norm_g"], W["w_in_r"], W["w_in_a"])

    shift_p = jnp.concatenate([shift_prev, jnp.zeros((B, 1, cr - nrc), F32)], axis=-1)
    prep = _rwkv_prep(proj, shift_p, B, T, cr, dr, W["mu_p"], W["w0"], W["a0"], W["k_k"], W["k_a"],
                      W["r_k"], W["w2p"], W["a2p"], W["g2p"])
    y_r, wkv_new = _wkv(prep, wkv_prev, B, T, dr, W["ln_w"], W["ln_b"])
    shift_new = proj.reshape(B, T, -1)[:, T - 1:, :nrc]

    da = H * ATT_HEAD
    if k_cache is None:
        y_a, k_new = _band_attention(proj, cr, B, T, H, W["q_g"], W["k_g"], bias_band)
        keep = min(ATT_CTX, T)
        k_new = k_new.reshape(B, keep, H, ATT_HEAD)
        v_new = proj.reshape(B, T, -1)[:, T - keep:, cr + 2 * da:].reshape(B, keep, H, ATT_HEAD)
    else:
        L = k_cache.shape[1]
        y_a, k_new = _cached_attention(proj, cr, B, T, H, k_cache.reshape(B, L * H, ATT_HEAD),
                                       v_cache.reshape(B, L * H, ATT_HEAD), W["q_g"], W["k_g"], bias_dec)
        k_new = k_new.reshape(B, T, H, ATT_HEAD)
        v_new = proj.reshape(B, T, -1)[:, :, cr + 2 * da:].reshape(B, T, H, ATT_HEAD)

    h = _outproj(y_r, y_a, W["w_out"], xf)
    act, conv_new = _ffn_up(h, W["ffn_norm_g"], W["w_up"], W["conv_w"], W["conv_b"], conv_prev, B, T)
    y = _ffn_down(act, W["w_down"], h)
    return y.reshape(B, T, D), shift_new, wkv_new, k_new, v_new, conv_new


def kernel(x_prompt, x_sample, state_rwkv_shift, state_rwkv_wkv, cache_attn_k, cache_attn_v,
           state_ffn_conv, attn_norm_g, w_in, rwkv_mu, rwkv_w0, rwkv_w2, rwkv_a0, rwkv_a2, rwkv_g2,
           rwkv_k_k, rwkv_k_a, rwkv_r_k, rwkv_ln_w, rwkv_ln_b, att_q_norm_g, att_k_norm_g,
           att_rel_bias, w_out, ffn_norm_g, ffn_w_up, ffn_conv_w, ffn_conv_b, ffn_w_down):
    depth = w_in.shape[0]
    B = x_prompt.shape[0]
    xp, xs = x_prompt, x_sample
    dec_t = x_sample.shape[1]
    L = cache_attn_k.shape[2]
    assert L == ATT_CTX
    dec_w = _round_up(L + dec_t, LANES)
    p_states = ([], [], [], [], [])
    s_states = ([], [], [], [], [])
    for l in range(depth):
        lp = (attn_norm_g[l], w_in[l], rwkv_mu[l], rwkv_w0[l], rwkv_w2[l], rwkv_a0[l], rwkv_a2[l],
              rwkv_g2[l], rwkv_k_k[l], rwkv_k_a[l], rwkv_r_k[l], rwkv_ln_w[l], rwkv_ln_b[l],
              att_q_norm_g[l], att_k_norm_g[l], att_rel_bias[l], w_out[l], ffn_norm_g[l],
              ffn_w_up[l], ffn_conv_w[l], ffn_conv_b[l], ffn_w_down[l])
        W = _layer_weights(lp)
        nrc, dr = W["nrc"], W["dr"]
        Fd = ffn_conv_b.shape[-1]
        bias_band, bias_dec = _bias_tiles(att_rel_bias[l], ATT_CTX, dec_t, dec_w)
        shift0 = jnp.zeros((B, 1, nrc), F32)
        wkv0 = jnp.zeros((B, dr // RWKV_HEAD, RWKV_HEAD, RWKV_HEAD), F32)
        conv0 = jnp.zeros((B, FFN_CONV - 1, Fd), F32)
        xp, *ps = _block(xp, shift0, wkv0, conv0, None, None, W, bias_band, bias_dec)
        xs, *ss = _block(xs, state_rwkv_shift[l], state_rwkv_wkv[l], state_ffn_conv[l],
                         cache_attn_k[l], cache_attn_v[l], W, bias_band, bias_dec)
        for i in range(5):
            p_states[i].append(ps[i])
            s_states[i].append(ss[i])
    p_out = [jnp.stack(t, axis=0) for t in p_states]
    s_out = [jnp.stack(t, axis=0) for t in s_states]
    return (xp, xs, *p_out, *s_out)
```
